```python
import math
import jax, jax.numpy as jnp
from jax import lax
import numpy as np

D_MODEL = 2048
BATCH = 8
SEQ = 2048
DEPTH = 4
DEC_BATCH = 32
DEC_SEQ = 64
PAST_LEN = 4096

CHUNK = 64
Q_BLOCK = 128
EPS = 1e-6
POOL_WIDTH = 512
POOL_WINDOWS = (2, 4, 8, 16)
POOL_GROUPS = len(POOL_WINDOWS)
POOL_GROUP_DIM = POOL_WIDTH // POOL_GROUPS
POOL_HIST = max(POOL_WINDOWS) - 1
N_DIFF_HEADS = 4
DIFF_HEAD_DIM = 128
DIFF_V_DIM = 2 * DIFF_HEAD_DIM
DIFF_QK = N_DIFF_HEADS * 2 * DIFF_HEAD_DIM
DIFF_WIDTH = N_DIFF_HEADS * DIFF_V_DIM
CONV_WIDTH = 512
CONV_K = 3
CONV_HIST = CONV_K - 1
MIX_WIDTH = POOL_WIDTH + DIFF_WIDTH + CONV_WIDTH
OFF_Q = POOL_WIDTH
OFF_K = OFF_Q + DIFF_QK
OFF_V = OFF_K + DIFF_QK
OFF_CB = OFF_V + DIFF_WIDTH
OFF_CC = OFF_CB + CONV_WIDTH
OFF_CX = OFF_CC + CONV_WIDTH
IN_COLS = OFF_CX + CONV_WIDTH
MEM_LEN = 256
N_MEM_HEADS = 4
MEM_HEAD_DIM = 128
MEM_WIDTH = N_MEM_HEADS * MEM_HEAD_DIM
D_FF = 5632

kernel_name = 'hybrid_stream_pool_diffattn_shortconv_step'

F32 = jnp.float32


def rmsnorm(x, g):
    xf = x.astype(F32)
    y = xf * lax.rsqrt(jnp.mean(xf * xf, axis=-1, keepdims=True) + EPS)
    return (y * g.astype(F32)).astype(x.dtype)


def swiglu(x, wg, wu, wd):
    return (jax.nn.silu(x @ wg) * (x @ wu)) @ wd


def alibi_slopes():
    return 2.0 ** (-8.0 * jnp.arange(1, N_DIFF_HEADS + 1, dtype=F32) / N_DIFF_HEADS)


def diff_attention_core(q, k, v, q_pos, k_pos, lam):
    s = jnp.einsum('bqhcd,bkhcd->bchqk', q, k).astype(F32) * (DIFF_HEAD_DIM ** -0.5)
    dist = jnp.abs(q_pos[:, None] - k_pos[None, :]).astype(F32)
    bias = -alibi_slopes()[:, None, None] * dist[None]
    allowed = (k_pos[None, :] // CHUNK) <= (q_pos[:, None] // CHUNK)
    s = jnp.where(allowed, s + bias, -jnp.inf)
    p = jax.nn.softmax(s, axis=-1)
    a = p[:, 0] - lam * p[:, 1]
    return jnp.einsum('bhqk,bkhe->bqhe', a.astype(v.dtype), v)


def pool_mixer(u, hist, pos, w_pool, scale):
    S = u.shape[1]
    ext = jnp.concatenate([hist.astype(u.dtype), u], axis=1)
    cs = jnp.cumsum(ext.astype(F32), axis=1)
    cs = jnp.concatenate([jnp.zeros_like(cs[:, :1]), cs], axis=1)
    end = cs[:, POOL_HIST + 1:POOL_HIST + 1 + S]
    uf = u.astype(F32)
    groups = []
    for g, win in enumerate(POOL_WINDOWS):
        c = slice(g * POOL_GROUP_DIM, (g + 1) * POOL_GROUP_DIM)
        wsum = end[..., c] - cs[:, POOL_HIST + 1 - win:POOL_HIST + 1 - win + S, c]
        cnt = jnp.minimum(pos + 1, win).astype(F32)[None, :, None]
        groups.append(wsum / cnt - uf[..., c])
    pooled = jnp.stack(groups, axis=2).astype(u.dtype)
    mixed = jnp.einsum('bsgc,gcd->bsgd', pooled, w_pool).reshape(u.shape)
    return mixed * scale, ext[:, -POOL_HIST:]


def conv_mixer(bg, cg, xh, hist, w_conv):
    S = xh.shape[1]
    cx = cg * xh
    ext = jnp.concatenate([hist.astype(cx.dtype), cx], axis=1)
    y = w_conv[0] * ext[:, 0:S]
    for j in range(1, CONV_K):
        y = y + w_conv[j] * ext[:, j:j + S]
    return bg * y, ext[:, -CONV_HIST:]


def mem_attention(h, mk, mv, w_mq, w_mo):
    B, S, _ = h.shape
    q = (h @ w_mq).reshape(B, S, N_MEM_HEADS, MEM_HEAD_DIM)
    s = jnp.einsum('bqhd,bkhd->bhqk', q, mk).astype(F32) * (MEM_HEAD_DIM ** -0.5)
    p = jax.nn.softmax(s, axis=-1).astype(mv.dtype)
    o = jnp.einsum('bhqk,bkhd->bqhd', p, mv).reshape(B, S, MEM_WIDTH)
    return o @ w_mo


def trunk_layer(x, pos, l, W, hist_pool, hist_conv, past_k, past_v, mem_k, mem_v):
    B, S, _ = x.shape
    lam_init = 0.8 - 0.6 * math.exp(-0.3 * l)
    f = swiglu(rmsnorm(x, W['ffn1_pre_g']), W['ffn1_wg'], W['ffn1_wu'], W['ffn1_wd'])
    x = x + 0.5 * rmsnorm(f, W['ffn1_post_g'])
    u = rmsnorm(x, W['mix_pre_g'])
    z = u @ W['w_in']
    y_pool, new_pool = pool_mixer(z[..., :OFF_Q], hist_pool, pos, W['pool_w'], W['pool_scale'])
    q = z[..., OFF_Q:OFF_K].reshape(B, S, N_DIFF_HEADS, 2, DIFF_HEAD_DIM)
    k_rows = z[..., OFF_K:OFF_V].reshape(B, S, N_DIFF_HEADS, DIFF_V_DIM)
    v_rows = z[..., OFF_V:OFF_CB].reshape(B, S, N_DIFF_HEADS, DIFF_V_DIM)
    lam = (jnp.exp(jnp.sum(W['diff_lq1'].astype(F32) * W['diff_lk1'].astype(F32)))
           - jnp.exp(jnp.sum(W['diff_lq2'].astype(F32) * W['diff_lk2'].astype(F32))) + lam_init)
    if past_k is None:
        k_all = k_rows.reshape(B, S, N_DIFF_HEADS, 2, DIFF_HEAD_DIM)
        v_all = v_rows

        def block(i):
            qb = lax.dynamic_slice_in_dim(q, i * Q_BLOCK, Q_BLOCK, axis=1)
            qp = i * Q_BLOCK + jnp.arange(Q_BLOCK, dtype=jnp.int32)
            return diff_attention_core(qb, k_all, v_all, qp, pos, lam)

        o = lax.map(block, jnp.arange(S // Q_BLOCK, dtype=jnp.int32))
        o = jnp.moveaxis(o, 0, 1).reshape(B, S, N_DIFF_HEADS, DIFF_V_DIM)
    else:
        P = past_k.shape[1]
        k_all = jnp.concatenate([past_k.astype(k_rows.dtype), k_rows], axis=1)
        k_all = k_all.reshape(B, P + S, N_DIFF_HEADS, 2, DIFF_HEAD_DIM)
        v_all = jnp.concatenate([past_v.astype(v_rows.dtype), v_rows], axis=1)
        k_pos = jnp.arange(P + S, dtype=jnp.int32)
        o = diff_attention_core(q, k_all, v_all, pos, k_pos, lam)
    y_diff = (rmsnorm(o, W['diff_subln_g']) * (1.0 - lam_init)).reshape(B, S, DIFF_WIDTH)
    y_conv, new_conv = conv_mixer(z[..., OFF_CB:OFF_CC], z[..., OFF_CC:OFF_CX], z[..., OFF_CX:],
                                  hist_conv, W['conv_w'])
    mixed = jnp.concatenate([y_pool, y_diff, y_conv], axis=-1) @ W['w_out']
    x = x + rmsnorm(mixed, W['mix_post_g'])
    m = mem_attention(rmsnorm(x, W['mem_pre_g']), mem_k, mem_v, W['w_mq'], W['w_mo'])
    x = x + rmsnorm(m, W['mem_post_g'])
    f = swiglu(rmsnorm(x, W['ffn2_pre_g']), W['ffn2_wg'], W['ffn2_wu'], W['ffn2_wd'])
    x = x + 0.5 * rmsnorm(f, W['ffn2_post_g'])
    return x, k_rows, v_rows, new_pool, new_conv


def setup_inputs(seed: int = 0) -> dict:
    key = jax.random.key(seed)
    ks = iter(jax.random.split(key, 48))

    def nrm(shape, s=1.0):
        return jax.random.normal(next(ks), shape, F32) * s

    def gain(shape):
        return 1.0 + 0.05 * jax.random.normal(next(ks), shape, F32)

    L, D = DEPTH, D_MODEL
    return {
        'x_prompt': nrm((BATCH, SEQ, D)),
        'x_sample': nrm((DEC_BATCH, DEC_SEQ, D)),
        'mem_prompt': nrm((BATCH, MEM_LEN, D)),
        'cache_diff_k': nrm((L, DEC_BATCH, PAST_LEN, N_DIFF_HEADS, DIFF_V_DIM)),
        'cache_diff_v': nrm((L, DEC_BATCH, PAST_LEN, N_DIFF_HEADS, DIFF_V_DIM)),
        'state_pool': nrm((L, DEC_BATCH, POOL_HIST, POOL_WIDTH)),
        'state_conv': nrm((L, DEC_BATCH, CONV_HIST, CONV_WIDTH)),
        'cache_mem_k': nrm((L, DEC_BATCH, MEM_LEN, N_MEM_HEADS, MEM_HEAD_DIM)),
        'cache_mem_v': nrm((L, DEC_BATCH, MEM_LEN, N_MEM_HEADS, MEM_HEAD_DIM)),
        'ffn1_pre_g': gain((L, D)),
        'ffn1_post_g': gain((L, D)),
        'ffn1_wg': nrm((L, D, D_FF), D ** -0.5),
        'ffn1_wu': nrm((L, D, D_FF), D ** -0.5),
        'ffn1_wd': nrm((L, D_FF, D), D_FF ** -0.5),
        'mix_pre_g': gain((L, D)),
        'mix_post_g': gain((L, D)),
        'w_in': nrm((L, D, IN_COLS), D ** -0.5),
        'pool_w': nrm((L, POOL_GROUPS, POOL_GROUP_DIM, POOL_GROUP_DIM), POOL_GROUP_DIM ** -0.5),
        'pool_scale': 1.0 + 0.1 * nrm((L, POOL_WIDTH)),
        'diff_lq1': nrm((L, DIFF_HEAD_DIM), 0.1),
        'diff_lk1': nrm((L, DIFF_HEAD_DIM), 0.1),
        'diff_lq2': nrm((L, DIFF_HEAD_DIM), 0.1),
        'diff_lk2': nrm((L, DIFF_HEAD_DIM), 0.1),
        'diff_subln_g': gain((L, DIFF_V_DIM)),
        'conv_w': nrm((L, CONV_K, CONV_WIDTH), CONV_K ** -0.5),
        'w_out': nrm((L, MIX_WIDTH, D), MIX_WIDTH ** -0.5),
        'mem_pre_g': gain((L, D)),
        'mem_post_g': gain((L, D)),
        'mem_kv_g': gain((L, D)),
        'w_mq': nrm((L, D, MEM_WIDTH), D ** -0.5),
        'w_mk': nrm((L, D, MEM_WIDTH), D ** -0.5),
        'w_mv': nrm((L, D, MEM_WIDTH), D ** -0.5),
        'w_mo': nrm((L, MEM_WIDTH, D), MEM_WIDTH ** -0.5),
        'ffn2_pre_g': gain((L, D)),
        'ffn2_post_g': gain((L, D)),
        'ffn2_wg': nrm((L, D, D_FF), D ** -0.5),
        'ffn2_wu': nrm((L, D, D_FF), D ** -0.5),
        'ffn2_wd': nrm((L, D_FF, D), D_FF ** -0.5),
    }


def reference(x_prompt, x_sample, mem_prompt, cache_diff_k, cache_diff_v, state_pool, state_conv,
              cache_mem_k, cache_mem_v,
              ffn1_pre_g, ffn1_post_g, ffn1_wg, ffn1_wu, ffn1_wd,
              mix_pre_g, mix_post_g, w_in, pool_w, pool_scale,
              diff_lq1, diff_lk1, diff_lq2, diff_lk2, diff_subln_g, conv_w, w_out,
              mem_pre_g, mem_post_g, mem_kv_g, w_mq, w_mk, w_mv, w_mo,
              ffn2_pre_g, ffn2_post_g, ffn2_wg, ffn2_wu, ffn2_wd):
    xp, xs = x_prompt, x_sample
    B, S, _ = xp.shape
    T = xs.shape[1]
    P = cache_diff_k.shape[2]
    pos_p = jnp.arange(S, dtype=jnp.int32)
    pos_s = P + jnp.arange(T, dtype=jnp.int32)
    zero_pool = jnp.zeros((B, POOL_HIST, POOL_WIDTH), xp.dtype)
    zero_conv = jnp.zeros((B, CONV_HIST, CONV_WIDTH), xp.dtype)
    kp_l, vp_l, pp_l, cp_l, mkp_l, mvp_l = [], [], [], [], [], []
    ks_l, vs_l, ps_l, cs_l = [], [], [], []
    for l in range(DEPTH):
        W = dict(ffn1_pre_g=ffn1_pre_g[l], ffn1_post_g=ffn1_post_g[l], ffn1_wg=ffn1_wg[l],
                 ffn1_wu=ffn1_wu[l], ffn1_wd=ffn1_wd[l], mix_pre_g=mix_pre_g[l], mix_post_g=mix_post_g[l],
                 w_in=w_in[l], pool_w=pool_w[l], pool_scale=pool_scale[l], diff_lq1=diff_lq1[l],
                 diff_lk1=diff_lk1[l], diff_lq2=diff_lq2[l], diff_lk2=diff_lk2[l],
                 diff_subln_g=diff_subln_g[l], conv_w=conv_w[l], w_out=w_out[l],
                 mem_pre_g=mem_pre_g[l], mem_post_g=mem_post_g[l], w_mq=w_mq[l], w_mo=w_mo[l],
                 ffn2_pre_g=ffn2_pre_g[l], ffn2_post_g=ffn2_post_g[l], ffn2_wg=ffn2_wg[l],
                 ffn2_wu=ffn2_wu[l], ffn2_wd=ffn2_wd[l])
        mn = rmsnorm(mem_prompt, mem_kv_g[l])
        mk_p = (mn @ w_mk[l]).reshape(B, MEM_LEN, N_MEM_HEADS, MEM_HEAD_DIM)
        mv_p = (mn @ w_mv[l]).reshape(B, MEM_LEN, N_MEM_HEADS, MEM_HEAD_DIM)
        xp, kp, vp, pp, cp = trunk_layer(xp, pos_p, l, W, zero_pool, zero_conv, None, None, mk_p, mv_p)
        xs, ks_, vs_, ps_, cs_ = trunk_layer(xs, pos_s, l, W, state_pool[l], state_conv[l],
                                             cache_diff_k[l], cache_diff_v[l],
                                             cache_mem_k[l], cache_mem_v[l])
        kp_l.append(kp); vp_l.append(vp); pp_l.append(pp); cp_l.append(cp)
        mkp_l.append(mk_p); mvp_l.append(mv_p)
        ks_l.append(ks_); vs_l.append(vs_); ps_l.append(ps_); cs_l.append(cs_)
    return (xp, xs,
            jnp.stack(kp_l), jnp.stack(vp_l), jnp.stack(pp_l), jnp.stack(cp_l),
            jnp.stack(mkp_l), jnp.stack(mvp_l),
            jnp.stack(ks_l), jnp.stack(vs_l), jnp.stack(ps_l), jnp.stack(cs_l))
```

```python
import functools
import math

import jax
import jax.numpy as jnp
from jax import lax
from jax.experimental import pallas as pl
from jax.experimental.pallas import tpu as pltpu

F32 = jnp.float32
BF16 = jnp.bfloat16

EPS = 1e-6
CHUNK = 64
POOL_WINDOWS = (2, 4, 8, 16)
POOL_GROUP_DIM = 128
POOL_WIDTH = POOL_GROUP_DIM * len(POOL_WINDOWS)
POOL_HIST = max(POOL_WINDOWS) - 1
POOL_HIST_PAD = 16
N_DIFF_HEADS = 4
DIFF_HEAD_DIM = 128
DIFF_V_DIM = 2 * DIFF_HEAD_DIM
DIFF_WIDTH = N_DIFF_HEADS * DIFF_V_DIM
CONV_WIDTH = 512
CONV_K = 3
CONV_HIST = CONV_K - 1
CONV_HIST_PAD = 8
N_MEM_HEADS = 4
MEM_HEAD_DIM = 128
MEM_WIDTH = N_MEM_HEADS * MEM_HEAD_DIM
OFF_Q = POOL_WIDTH
OFF_K = OFF_Q + DIFF_WIDTH
OFF_V = OFF_K + DIFF_WIDTH
OFF_CB = OFF_V + DIFF_WIDTH
OFF_CC = OFF_CB + CONV_WIDTH
OFF_CX = OFF_CC + CONV_WIDTH

V7X_VMEM_LIMIT_BYTES = 56 * 1024 * 1024


def _params(n_axes):
    return pltpu.CompilerParams(dimension_semantics=("arbitrary",) * n_axes,
                                vmem_limit_bytes=V7X_VMEM_LIMIT_BYTES)


def _rms(xf, g):
    ms = jnp.mean(xf * xf, axis=-1, keepdims=True)
    return xf * lax.rsqrt(ms + EPS) * g


def _dot(a, b):
    return jnp.dot(a, b, preferred_element_type=F32)


def _dot_nt(a, b):
    return lax.dot_general(a, b, (((1,), (1,)), ((), ())), preferred_element_type=F32)


def _ffn_kernel(x_ref, gpre_ref, wg_ref, wu_ref, wd_ref, gpost_ref, y_ref, xn_ref, *, n_ff_steps):
    j = pl.program_id(1)

    @pl.when(j == 0)
    def _():
        xn_ref[...] = _rms(x_ref[...], gpre_ref[...]).astype(BF16)

    xn = xn_ref[...]
    gate = _dot(xn, wg_ref[...])
    up = _dot(xn, wu_ref[...])
    h = (gate * jax.nn.sigmoid(gate)) * up
    part = _dot(h.astype(BF16), wd_ref[...])

    @pl.when(j == 0)
    def _():
        y_ref[...] = part

    @pl.when(j > 0)
    def _():
        y_ref[...] += part

    @pl.when(j == n_ff_steps - 1)
    def _():
        y_ref[...] = x_ref[...] + 0.5 * _rms(y_ref[...], gpost_ref[...])


def _ffn(x, g_pre, wg, wu, wd, g_post, layer, *, tm, tf):
    n, d = x.shape
    f = wg.shape[-1]
    tm, tf = min(tm, n), min(tf, f)
    assert n % tm == 0 and f % tf == 0
    return pl.pallas_call(
        functools.partial(_ffn_kernel, n_ff_steps=f // tf),
        grid=(n // tm, f // tf),
        in_specs=[
            pl.BlockSpec((tm, d), lambda i, j: (i, 0)),
            pl.BlockSpec((None, 1, d), lambda i, j: (layer, 0, 0)),
            pl.BlockSpec((None, d, tf), lambda i, j: (layer, 0, j)),
            pl.BlockSpec((None, d, tf), lambda i, j: (layer, 0, j)),
            pl.BlockSpec((None, tf, d), lambda i, j: (layer, j, 0)),
            pl.BlockSpec((None, 1, d), lambda i, j: (layer, 0, 0)),
        ],
        out_specs=pl.BlockSpec((tm, d), lambda i, j: (i, 0)),
        out_shape=jax.ShapeDtypeStruct((n, d), F32),
        scratch_shapes=[pltpu.VMEM((tm, d), BF16)],
        compiler_params=_params(2),
        name="ffn",
    )(x, g_pre, wg, wu, wd, g_post)


def _norm_matmul_kernel(x_ref, g_ref, w_ref, o_ref, xn_ref):
    @pl.when(pl.program_id(1) == 0)
    def _():
        xn_ref[...] = _rms(x_ref[...], g_ref[...]).astype(BF16)

    o_ref[...] = _dot(xn_ref[...], w_ref[...])


def _norm_matmul(x, g, w, layer, *, tm, tn):
    n, d = x.shape
    m = w.shape[-1]
    tm, tn = min(tm, n), min(tn, m)
    assert n % tm == 0 and m % tn == 0
    return pl.pallas_call(
        _norm_matmul_kernel,
        grid=(n // tm, m // tn),
        in_specs=[
            pl.BlockSpec((tm, d), lambda i, j: (i, 0)),
            pl.BlockSpec((None, 1, d), lambda i, j: (layer, 0, 0)),
            pl.BlockSpec((None, d, tn), lambda i, j: (layer, 0, j)),
        ],
        out_specs=pl.BlockSpec((tm, tn), lambda i, j: (i, j)),
        out_shape=jax.ShapeDtypeStruct((n, m), F32),
        scratch_shapes=[pltpu.VMEM((tm, d), BF16)],
        compiler_params=_params(2),
        name="norm_matmul",
    )(x, g, w)


def _mem_kv_kernel(x_ref, g_ref, wk_ref, wv_ref, k_ref, v_ref):
    xn = _rms(x_ref[...], g_ref[...]).astype(BF16)
    k_ref[...] = _dot(xn, wk_ref[...])
    v_ref[...] = _dot(xn, wv_ref[...])


def _mem_kv(mem, g, wk, wv, *, tm):
    r, d = mem.shape
    n_layers = wk.shape[0]
    tm = min(tm, r)
    assert r % tm == 0
    out = jax.ShapeDtypeStruct((n_layers, r, MEM_WIDTH), F32)
    return pl.pallas_call(
        _mem_kv_kernel,
        grid=(n_layers, r // tm),
        in_specs=[
            pl.BlockSpec((tm, d), lambda l, i: (i, 0)),
            pl.BlockSpec((None, 1, d), lambda l, i: (l, 0, 0)),
            pl.BlockSpec((None, d, MEM_WIDTH), lambda l, i: (l, 0, 0)),
            pl.BlockSpec((None, d, MEM_WIDTH), lambda l, i: (l, 0, 0)),
        ],
        out_specs=[pl.BlockSpec((None, tm, MEM_WIDTH), lambda l, i: (l, i, 0))] * 2,
        out_shape=[out, out],
        compiler_params=_params(2),
        name="mem_kv",
    )(mem, g, wk, wv)


def _pool_conv_kernel(zp_ref, cb_ref, cc_ref, cx_ref, hp_ref, hc_ref, pw_ref, ps_ref, cw_ref,
                      yp_ref, yc_ref, nc_ref, extp_ref, extc_ref, *, seq, pos0, rows):
    extp_ref[0:POOL_HIST_PAD, :] = hp_ref[...]
    extp_ref[POOL_HIST_PAD:POOL_HIST_PAD + seq, :] = zp_ref[...]
    extc_ref[0:CONV_HIST_PAD, :] = hc_ref[...]
    extc_ref[CONV_HIST_PAD:CONV_HIST_PAD + seq, :] = cc_ref[...] * cx_ref[...]
    nc_ref[...] = extc_ref[CONV_HIST_PAD + seq - CONV_HIST:CONV_HIST_PAD + seq, :]

    for r0 in range(0, seq, rows):
        pos = pos0 + r0 + lax.broadcasted_iota(jnp.int32, (rows, 1), 0)
        for g, win in enumerate(POOL_WINDOWS):
            c = slice(g * POOL_GROUP_DIM, (g + 1) * POOL_GROUP_DIM)
            base = POOL_HIST_PAD + r0
            tok = extp_ref[base:base + rows, c]
            wsum = tok
            for k in range(1, win):
                wsum = wsum + extp_ref[base - k:base - k + rows, c]
            cnt = jnp.minimum(pos + 1, win).astype(F32)
            pooled = wsum / cnt - tok
            mixed = _dot(pooled.astype(BF16), pw_ref[g])
            yp_ref[r0:r0 + rows, c] = (mixed * ps_ref[:, c]).astype(BF16)
        base = CONV_HIST_PAD + r0
        y = cw_ref[0:1, :] * extc_ref[base - 2:base - 2 + rows, :]
        y = y + cw_ref[1:2, :] * extc_ref[base - 1:base - 1 + rows, :]
        y = y + cw_ref[2:3, :] * extc_ref[base:base + rows, :]
        yc_ref[r0:r0 + rows, :] = (cb_ref[r0:r0 + rows, :] * y).astype(BF16)


def _pool_conv(z3, hist_pool, hist_conv, pool_w, pool_scale, conv_w, layer, *, pos0):
    b, seq, _ = z3.shape
    assert seq >= CONV_HIST and seq % 8 == 0
    rows = min(seq, 256)
    assert seq % rows == 0
    col = lambda off: (lambda i: (i, 0, off // 512))
    seq_block = lambda off: pl.BlockSpec((None, seq, 512), col(off))
    return pl.pallas_call(
        functools.partial(_pool_conv_kernel, seq=seq, pos0=pos0, rows=rows),
        grid=(b,),
        in_specs=[
            seq_block(0), seq_block(OFF_CB), seq_block(OFF_CC), seq_block(OFF_CX),
            pl.BlockSpec((None, POOL_HIST_PAD, POOL_WIDTH), lambda i: (i, 0, 0)),
            pl.BlockSpec((None, CONV_HIST_PAD, CONV_WIDTH), lambda i: (i, 0, 0)),
            pl.BlockSpec((None, len(POOL_WINDOWS), POOL_GROUP_DIM, POOL_GROUP_DIM),
                         lambda i: (layer, 0, 0, 0)),
            pl.BlockSpec((None, 1, POOL_WIDTH), lambda i: (layer, 0, 0)),
            pl.BlockSpec((None, CONV_K, CONV_WIDTH), lambda i: (layer, 0, 0)),
        ],
        out_specs=[
            pl.BlockSpec((None, seq, POOL_WIDTH), lambda i: (i, 0, 0)),
            pl.BlockSpec((None, seq, CONV_WIDTH), lambda i: (i, 0, 0)),
            pl.BlockSpec((None, CONV_HIST, CONV_WIDTH), lambda i: (i, 0, 0)),
        ],
        out_shape=[
            jax.ShapeDtypeStruct((b, seq, POOL_WIDTH), BF16),
            jax.ShapeDtypeStruct((b, seq, CONV_WIDTH), BF16),
            jax.ShapeDtypeStruct((b, CONV_HIST, CONV_WIDTH), F32),
        ],
        scratch_shapes=[pltpu.VMEM((POOL_HIST_PAD + seq, POOL_WIDTH), F32),
                        pltpu.VMEM((CONV_HIST_PAD + seq, CONV_WIDTH), F32)],
        compiler_params=_params(1),
        name="pool_conv",
    )(z3, z3, z3, z3, hist_pool, hist_conv, pool_w, pool_scale, conv_w)


def _diff_lambda(lq1_ref, lk1_ref, lq2_ref, lk2_ref, lam_init):
    a = jnp.sum(lq1_ref[...] * lk1_ref[...], axis=-1, keepdims=True)
    b = jnp.sum(lq2_ref[...] * lk2_ref[...], axis=-1, keepdims=True)
    return jnp.exp(a) - jnp.exp(b) + lam_init


def _online_softmax_step(s, v, m_ref, l_ref, acc_ref, idx):
    m_old = m_ref[idx]
    m_new = jnp.maximum(m_old, jnp.max(s, axis=-1, keepdims=True))
    alpha = jnp.exp(m_old - m_new)
    p = jnp.exp(s - m_new)
    l_ref[idx] = alpha * l_ref[idx] + jnp.sum(p, axis=-1, keepdims=True)
    acc_ref[idx] = alpha * acc_ref[idx] + _dot(p.astype(BF16), v)
    m_ref[idx] = m_new


def _diff_combine(m_ref, l_ref, acc_ref, idx, lam, g, lam_init):
    o = acc_ref[idx] / l_ref[idx] - lam * (acc_ref[idx + 1] / l_ref[idx + 1])
    return _rms(o, g) * (1.0 - lam_init)


def _diff_prompt_kernel(slopes_ref, q_ref, k_ref, v_ref, lq1_ref, lk1_ref, lq2_ref, lk2_ref, g_ref,
                        o_ref, kb_ref, vb_ref, m_ref, l_ref, acc_ref, *, tq, lam_init):
    h = pl.program_id(1)
    qi = pl.program_id(2)

    @pl.when(qi == 0)
    def _():
        kb_ref[...] = k_ref[...].astype(BF16)
        vb_ref[...] = v_ref[...].astype(BF16)

    m_ref[...] = jnp.full(m_ref.shape, -jnp.inf, F32)
    l_ref[...] = jnp.zeros(l_ref.shape, F32)
    acc_ref[...] = jnp.zeros(acc_ref.shape, F32)

    q = q_ref[...]
    qs = (q[:, :DIFF_HEAD_DIM].astype(BF16), q[:, DIFF_HEAD_DIM:].astype(BF16))
    scale = DIFF_HEAD_DIM ** -0.5
    slope = slopes_ref[h]
    ii = lax.broadcasted_iota(jnp.int32, (tq, tq), 0)
    jj = lax.broadcasted_iota(jnp.int32, (tq, tq), 1)

    def tile(kt, diagonal):
        k0 = pl.multiple_of(kt * tq, tq)
        kk = kb_ref[pl.ds(k0, tq), :]
        vv = vb_ref[pl.ds(k0, tq), :]
        if diagonal:
            dist = jnp.abs(ii - jj)
            allowed = (jj // CHUNK) <= (ii // CHUNK)
        else:
            dist = ii - jj + (qi - kt) * tq
        bias = -slope * dist.astype(F32)
        for c in range(2):
            s = _dot_nt(qs[c], kk[:, c * DIFF_HEAD_DIM:(c + 1) * DIFF_HEAD_DIM]) * scale + bias
            if diagonal:
                s = jnp.where(allowed, s, -jnp.inf)
            _online_softmax_step(s, vv, m_ref, l_ref, acc_ref, c)

    def body(kt, carry):
        tile(kt, False)
        return carry

    lax.fori_loop(0, qi, body, 0)
    tile(qi, True)

    lam = _diff_lambda(lq1_ref, lk1_ref, lq2_ref, lk2_ref, lam_init)
    o_ref[...] = _diff_combine(m_ref, l_ref, acc_ref, 0, lam, g_ref[...], lam_init).astype(BF16)


def _diff_lambda_specs(layer, n_axes):
    idx = {2: lambda a, b: (layer, 0, 0), 3: lambda a, b, c: (layer, 0, 0)}[n_axes]
    return [pl.BlockSpec((None, 1, DIFF_HEAD_DIM), idx)] * 4 + [pl.BlockSpec((None, 1, DIFF_V_DIM), idx)]


def _diff_attn_prompt(z3, slopes, lq1, lk1, lq2, lk2, subln_g, layer, *, tq, lam_init):
    b, seq, _ = z3.shape
    assert seq % tq == 0 and tq % CHUNK == 0
    col = lambda off: off // DIFF_V_DIM
    return pl.pallas_call(
        functools.partial(_diff_prompt_kernel, tq=tq, lam_init=lam_init),
        grid=(b, N_DIFF_HEADS, seq // tq),
        in_specs=[
            pl.BlockSpec(memory_space=pltpu.SMEM),
            pl.BlockSpec((None, tq, DIFF_V_DIM), lambda i, h, q: (i, q, col(OFF_Q) + h)),
            pl.BlockSpec((None, seq, DIFF_V_DIM), lambda i, h, q: (i, 0, col(OFF_K) + h)),
            pl.BlockSpec((None, seq, DIFF_V_DIM), lambda i, h, q: (i, 0, col(OFF_V) + h)),
        ] + _diff_lambda_specs(layer, 3),
        out_specs=pl.BlockSpec((None, tq, DIFF_V_DIM), lambda i, h, q: (i, q, h)),
        out_shape=jax.ShapeDtypeStruct((b, seq, DIFF_WIDTH), BF16),
        scratch_shapes=[
            pltpu.VMEM((seq, DIFF_V_DIM), BF16),
            pltpu.VMEM((seq, DIFF_V_DIM), BF16),
            pltpu.VMEM((2, tq, 1), F32),
            pltpu.VMEM((2, tq, 1), F32),
            pltpu.VMEM((2, tq, DIFF_V_DIM), F32),
        ],
        compiler_params=_params(3),
        name="diff_attn_prompt",
    )(slopes, z3, z3, z3, lq1, lk1, lq2, lk2, subln_g)


def _diff_sample_kernel(slopes_ref, q_ref, kn_ref, vn_ref, kc_ref, vc_ref,
                        lq1_ref, lk1_ref, lq2_ref, lk2_ref, g_ref,
                        o_ref, m_ref, l_ref, acc_ref, *, past, seq, tk, lam_init):
    kt = pl.program_id(1)
    n_cache_tiles = past // tk

    @pl.when(kt == 0)
    def _():
        m_ref[...] = jnp.full(m_ref.shape, -jnp.inf, F32)
        l_ref[...] = jnp.zeros(l_ref.shape, F32)
        acc_ref[...] = jnp.zeros(acc_ref.shape, F32)

    scale = DIFF_HEAD_DIM ** -0.5

    def process(k_ref, v_ref, n_keys, new_keys):
        ii = lax.broadcasted_iota(jnp.int32, (seq, n_keys), 0)
        jj = lax.broadcasted_iota(jnp.int32, (seq, n_keys), 1)
        if new_keys:
            dist = jnp.abs(ii - jj)
        else:
            dist = ii - jj + (past - kt * tk)
        dist = dist.astype(F32)
        for h in range(N_DIFF_HEADS):
            bias = -slopes_ref[h] * dist
            vv = v_ref[:, h * DIFF_V_DIM:(h + 1) * DIFF_V_DIM].astype(BF16)
            for c in range(2):
                lo = h * DIFF_V_DIM + c * DIFF_HEAD_DIM
                qc = q_ref[:, lo:lo + DIFF_HEAD_DIM].astype(BF16)
                kc = k_ref[:, lo:lo + DIFF_HEAD_DIM].astype(BF16)
                s = _dot_nt(qc, kc) * scale + bias
                _online_softmax_step(s, vv, m_ref, l_ref, acc_ref, 2 * h + c)

    @pl.when(kt < n_cache_tiles)
    def _():
        process(kc_ref, vc_ref, tk, False)

    @pl.when(kt == n_cache_tiles)
    def _():
        process(kn_ref, vn_ref, seq, True)
        lam = _diff_lambda(lq1_ref, lk1_ref, lq2_ref, lk2_ref, lam_init)
        for h in range(N_DIFF_HEADS):
            y = _diff_combine(m_ref, l_ref, acc_ref, 2 * h, lam, g_ref[...], lam_init)
            o_ref[:, h * DIFF_V_DIM:(h + 1) * DIFF_V_DIM] = y.astype(BF16)


def _diff_attn_sample(q3, k3, v3, cache_k, cache_v, slopes, lq1, lk1, lq2, lk2, subln_g, layer, *, tk, lam_init):
    b, seq, _ = q3.shape
    past = cache_k.shape[2]
    assert past % tk == 0 and past % CHUNK == 0 and seq <= CHUNK
    n_cache_tiles = past // tk
    cache_spec = pl.BlockSpec((None, None, tk, DIFF_WIDTH),
                              lambda i, k: (layer, i, jnp.minimum(k, n_cache_tiles - 1), 0))
    new_spec = pl.BlockSpec((None, seq, DIFF_WIDTH), lambda i, k: (i, 0, 0))
    return pl.pallas_call(
        functools.partial(_diff_sample_kernel, past=past, seq=seq, tk=tk, lam_init=lam_init),
        grid=(b, n_cache_tiles + 1),
        in_specs=[
            pl.BlockSpec(memory_space=pltpu.SMEM),
            new_spec, new_spec, new_spec,
            cache_spec, cache_spec,
        ] + _diff_lambda_specs(layer, 2),
        out_specs=pl.BlockSpec((None, seq, DIFF_WIDTH), lambda i, k: (i, 0, 0)),
        out_shape=jax.ShapeDtypeStruct((b, seq, DIFF_WIDTH), BF16),
        scratch_shapes=[
            pltpu.VMEM((2 * N_DIFF_HEADS, seq, 1), F32),
            pltpu.VMEM((2 * N_DIFF_HEADS, seq, 1), F32),
            pltpu.VMEM((2 * N_DIFF_HEADS, seq, DIFF_V_DIM), F32),
        ],
        compiler_params=_params(2),
        name="diff_attn_sample",
    )(slopes, q3, k3, v3, cache_k, cache_v, lq1, lk1, lq2, lk2, subln_g)


def _out_proj_kernel(yp_ref, yd_ref, yc_ref, w_ref, x_ref, g_ref, o_ref):
    m = _dot(yp_ref[...], w_ref[0:POOL_WIDTH, :])
    m = m + _dot(yd_ref[...], w_ref[POOL_WIDTH:POOL_WIDTH + DIFF_WIDTH, :])
    m = m + _dot(yc_ref[...], w_ref[POOL_WIDTH + DIFF_WIDTH:, :])
    o_ref[...] = x_ref[...] + _rms(m, g_ref[...])


def _out_proj(yp, yd, yc, w_out, x, g_post, layer, *, tm):
    n, d = x.shape
    tm = min(tm, n)
    assert n % tm == 0
    row = lambda width: pl.BlockSpec((tm, width), lambda i: (i, 0))
    return pl.pallas_call(
        _out_proj_kernel,
        grid=(n // tm,),
        in_specs=[
            row(POOL_WIDTH), row(DIFF_WIDTH), row(CONV_WIDTH),
            pl.BlockSpec((None, w_out.shape[1], d), lambda i: (layer, 0, 0)),
            row(d),
            pl.BlockSpec((None, 1, d), lambda i: (layer, 0, 0)),
        ],
        out_specs=row(d),
        out_shape=jax.ShapeDtypeStruct((n, d), F32),
        compiler_params=_params(1),
        name="out_proj",
    )(yp, yd, yc, w_out, x, g_post)


def _mem_attn_kernel(x_ref, gpre_ref, wq_ref, mk_ref, mv_ref, wo_ref, gpost_ref, o_ref):
    x = x_ref[...]
    q = _dot(_rms(x, gpre_ref[...]).astype(BF16), wq_ref[...])
    scale = MEM_HEAD_DIM ** -0.5
    heads = []
    for h in range(N_MEM_HEADS):
        c = slice(h * MEM_HEAD_DIM, (h + 1) * MEM_HEAD_DIM)
        s = _dot_nt(q[:, c].astype(BF16), mk_ref[:, c].astype(BF16)) * scale
        p = jnp.exp(s - jnp.max(s, axis=-1, keepdims=True))
        p = p / jnp.sum(p, axis=-1, keepdims=True)
        heads.append(_dot(p.astype(BF16), mv_ref[:, c].astype(BF16)))
    o = jnp.concatenate(heads, axis=-1).astype(BF16)
    o_ref[...] = x + _rms(_dot(o, wo_ref[...]), gpost_ref[...])


def _mem_attn(x3, g_pre, w_mq, mk, mv, w_mo, g_post, layer, mem_layer, *, tm):
    b, seq, d = x3.shape
    mem_len = mk.shape[2]
    tm = min(tm, seq)
    assert seq % tm == 0
    mem_spec = pl.BlockSpec((None, None, mem_len, MEM_WIDTH), lambda i, r: (mem_layer, i, 0, 0))
    gain = pl.BlockSpec((None, 1, d), lambda i, r: (layer, 0, 0))
    return pl.pallas_call(
        _mem_attn_kernel,
        grid=(b, seq // tm),
        in_specs=[
            pl.BlockSpec((None, tm, d), lambda i, r: (i, r, 0)),
            gain,
            pl.BlockSpec((None, d, MEM_WIDTH), lambda i, r: (layer, 0, 0)),
            mem_spec, mem_spec,
            pl.BlockSpec((None, MEM_WIDTH, d), lambda i, r: (layer, 0, 0)),
            gain,
        ],
        out_specs=pl.BlockSpec((None, tm, d), lambda i, r: (i, r, 0)),
        out_shape=jax.ShapeDtypeStruct((b, seq, d), F32),
        compiler_params=_params(2),
        name="mem_attn",
    )(x3, g_pre, w_mq, mk, mv, w_mo, g_post)


def _pad_front(hist, rows):
    return jnp.pad(hist, ((0, 0), (rows - hist.shape[1], 0), (0, 0)))


def kernel(x_prompt, x_sample, mem_prompt, cache_diff_k, cache_diff_v, state_pool, state_conv, cache_mem_k, cache_mem_v, ffn1_pre_g, ffn1_post_g, ffn1_wg, ffn1_wu, ffn1_wd, mix_pre_g, mix_post_g, w_in, pool_w, pool_scale, diff_lq1, diff_lk1, diff_lq2, diff_lk2, diff_subln_g, conv_w, w_out, mem_pre_g, mem_post_g, mem_kv_g, w_mq, w_mk, w_mv, w_mo, ffn2_pre_g, ffn2_post_g, ffn2_wg, ffn2_wu, ffn2_wd):
    n_layers = w_in.shape[0]
    bp, sp, d = x_prompt.shape
    bs, ss, _ = x_sample.shape
    past = cache_diff_k.shape[2]
    mem_len = mem_prompt.shape[1]
    assert sp >= POOL_HIST and ss >= POOL_HIST

    bf = lambda w: w.astype(BF16)
    row = lambda v: v.reshape(n_layers, 1, v.shape[-1])
    ffn1 = (row(ffn1_pre_g), bf(ffn1_wg), bf(ffn1_wu), bf(ffn1_wd), row(ffn1_post_g))
    ffn2 = (row(ffn2_pre_g), bf(ffn2_wg), bf(ffn2_wu), bf(ffn2_wd), row(ffn2_post_g))
    mix_pre_g, mix_post_g = row(mix_pre_g), row(mix_post_g)
    mem_pre_g, mem_post_g, mem_kv_g = row(mem_pre_g), row(mem_post_g), row(mem_kv_g)
    w_in, w_out, w_mq, w_mk, w_mv, w_mo, pool_w = map(bf, (w_in, w_out, w_mq, w_mk, w_mv, w_mo, pool_w))
    pool_scale = row(pool_scale)
    lam_vecs = tuple(row(v) for v in (diff_lq1, diff_lk1, diff_lq2, diff_lk2, diff_subln_g))
    slopes = jnp.asarray([2.0 ** (-8.0 * (h + 1) / N_DIFF_HEADS) for h in range(N_DIFF_HEADS)], F32)

    mk_p, mv_p = _mem_kv(mem_prompt.reshape(bp * mem_len, d), mem_kv_g, w_mk, w_mv, tm=1024)
    mk_p = mk_p.reshape(n_layers, bp, mem_len, MEM_WIDTH)
    mv_p = mv_p.reshape(n_layers, bp, mem_len, MEM_WIDTH)
    mk_s = cache_mem_k.reshape(n_layers, bs, mem_len, MEM_WIDTH)
    mv_s = cache_mem_v.reshape(n_layers, bs, mem_len, MEM_WIDTH)
    cache_k = cache_diff_k.reshape(n_layers, bs, past, DIFF_WIDTH)
    cache_v = cache_diff_v.reshape(n_layers, bs, past, DIFF_WIDTH)

    zero_pool = jnp.zeros((bp, POOL_HIST_PAD, POOL_WIDTH), F32)
    zero_conv = jnp.zeros((bp, CONV_HIST_PAD, CONV_WIDTH), F32)

    xp = x_prompt.reshape(bp * sp, d)
    xs = x_sample.reshape(bs * ss, d)
    outs = {name: [] for name in ("kp", "vp", "pp", "cp", "ks", "vs", "ps", "cs")}

    def layer_step(x, b, seq, l, hist_pool, hist_conv, pos0, diff_attn, mk, mv, mem_tm):
        lam_init = 0.8 - 0.6 * math.exp(-0.3 * l)
        x = _ffn(x, *ffn1, l, tm=512, tf=512)
        z = _norm_matmul(x, mix_pre_g, w_in, l, tm=1024, tn=512)
        z3 = z.reshape(b, seq, z.shape[-1])
        y_pool, y_conv, new_conv = _pool_conv(z3, hist_pool, hist_conv, pool_w, pool_scale, conv_w, l, pos0=pos0)
        y_diff = diff_attn(z3, l, lam_init)
        x = _out_proj(y_pool.reshape(b * seq, -1), y_diff.reshape(b * seq, -1), y_conv.reshape(b * seq, -1),
                      w_out, x, mix_post_g, l, tm=512)
        x = _mem_attn(x.reshape(b, seq, d), mem_pre_g, w_mq, mk, mv, w_mo, mem_post_g, l, l, tm=mem_tm)
        x = _ffn(x.reshape(b * seq, d), *ffn2, l, tm=512, tf=512)
        k_rows = z3[:, :, OFF_K:OFF_V].reshape(b, seq, N_DIFF_HEADS, DIFF_V_DIM)
        v_rows = z3[:, :, OFF_V:OFF_CB].reshape(b, seq, N_DIFF_HEADS, DIFF_V_DIM)
        new_pool = z3[:, seq - POOL_HIST:, :POOL_WIDTH]
        return x, k_rows, v_rows, new_pool, new_conv

    def prompt_attn(z3, l, lam_init):
        return _diff_attn_prompt(z3, slopes, *lam_vecs, l, tq=256, lam_init=lam_init)

    def sample_attn(z3, l, lam_init):
        q3, k3, v3 = (z3[:, :, off:off + DIFF_WIDTH] for off in (OFF_Q, OFF_K, OFF_V))
        return _diff_attn_sample(q3, k3, v3, cache_k, cache_v, slopes, *lam_vecs, l,
                                 tk=min(1024, past), lam_init=lam_init)

    for l in range(n_layers):
        xp, kp, vp, pp, cp = layer_step(xp, bp, sp, l, zero_pool, zero_conv, 0, prompt_attn, mk_p, mv_p, 512)
        xs, ks, vs, ps, cs = layer_step(xs, bs, ss, l, _pad_front(state_pool[l], POOL_HIST_PAD),
                                        _pad_front(state_conv[l], CONV_HIST_PAD), past, sample_attn,
                                        mk_s, mv_s, ss)
        for name, val in zip(("kp", "vp", "pp", "cp", "ks", "vs", "ps", "cs"), (kp, vp, pp, cp, ks, vs, ps, cs)):
            outs[name].append(val)

    st = lambda name: jnp.stack(outs[name])
    head_shape = (n_layers, bp, mem_len, N_MEM_HEADS, MEM_HEAD_DIM)
    return (xp.reshape(bp, sp, d), xs.reshape(bs, ss, d),
            st("kp"), st("vp"), st("pp"), st("cp"),
            mk_p.reshape(head_shape), mv_p.reshape(head_shape),
            st("ks"), st("vs"), st("ps"), st("cs"))
```

```python
import functools
import math

import jax
import jax.numpy as jnp
from jax import lax
from jax.experimental import pallas as pl
from jax.experimental.pallas import tpu as pltpu

F32 = jnp.float32
BF16 = jnp.bfloat16

EPS = 1e-6
CHUNK = 64
POOL_WINDOWS = (2, 4, 8, 16)
POOL_GROUP_DIM = 128
POOL_WIDTH = POOL_GROUP_DIM * len(POOL_WINDOWS)
POOL_HIST = max(POOL_WINDOWS) - 1
POOL_HIST_PAD = 16
N_DIFF_HEADS = 4
DIFF_HEAD_DIM = 128
DIFF_V_DIM = 2 * DIFF_HEAD_DIM
DIFF_WIDTH = N_DIFF_HEADS * DIFF_V_DIM
CONV_WIDTH = 512
CONV_K = 3
CONV_HIST = CONV_K - 1
CONV_HIST_PAD = 8
N_MEM_HEADS = 4
MEM_HEAD_DIM = 128
MEM_WIDTH = N_MEM_HEADS * MEM_HEAD_DIM
OFF_Q = POOL_WIDTH
OFF_K = OFF_Q + DIFF_WIDTH
OFF_V = OFF_K + DIFF_WIDTH
OFF_CB = OFF_V + DIFF_WIDTH
OFF_CC = OFF_CB + CONV_WIDTH
OFF_CX = OFF_CC + CONV_WIDTH

V7X_VMEM_LIMIT_BYTES = 56 * 1024 * 1024


def _params(n_axes):
    return pltpu.CompilerParams(dimension_semantics=("arbitrary",) * n_axes,
                                vmem_limit_bytes=V7X_VMEM_LIMIT_BYTES)


def _rms(xf, g):
    ms = jnp.mean(xf * xf, axis=-1, keepdims=True)
    return xf * lax.rsqrt(ms + EPS) * g


def _dot(a, b):
    return jnp.dot(a, b, preferred_element_type=F32)


def _dot_nt(a, b):
    return lax.dot_general(a, b, (((1,), (1,)), ((), ())), preferred_element_type=F32)


def _ffn_kernel(x_ref, gpre_ref, wg_ref, wu_ref, wd_ref, gpost_ref, y_ref, xn_ref, *, n_ff_steps):
    j = pl.program_id(1)

    @pl.when(j == 0)
    def _():
        xn_ref[...] = _rms(x_ref[...], gpre_ref[...]).astype(BF16)
        y_ref[...] = jnp.zeros(y_ref.shape, F32)

    xn = xn_ref[...]
    gate = _dot(xn, wg_ref[...])
    up = _dot(xn, wu_ref[...])
    h = (gate * jax.nn.sigmoid(gate)) * up
    y_ref[...] += _dot(h.astype(BF16), wd_ref[...])

    @pl.when(j == n_ff_steps - 1)
    def _():
        y_ref[...] = x_ref[...] + 0.5 * _rms(y_ref[...], gpost_ref[...])


def _ffn(x, g_pre, wg, wu, wd, g_post, layer, *, tm, tf):
    n, d = x.shape
    f = wg.shape[-1]
    tm, tf = min(tm, n), min(tf, f)
    assert n % tm == 0 and f % tf == 0
    return pl.pallas_call(
        functools.partial(_ffn_kernel, n_ff_steps=f // tf),
        grid=(n // tm, f // tf),
        in_specs=[
            pl.BlockSpec((tm, d), lambda i, j: (i, 0)),
            pl.BlockSpec((None, 1, d), lambda i, j: (layer, 0, 0)),
            pl.BlockSpec((None, d, tf), lambda i, j: (layer, 0, j)),
            pl.BlockSpec((None, d, tf), lambda i, j: (layer, 0, j)),
            pl.BlockSpec((None, tf, d), lambda i, j: (layer, j, 0)),
            pl.BlockSpec((None, 1, d), lambda i, j: (layer, 0, 0)),
        ],
        out_specs=pl.BlockSpec((tm, d), lambda i, j: (i, 0)),
        out_shape=jax.ShapeDtypeStruct((n, d), F32),
        scratch_shapes=[pltpu.VMEM((tm, d), BF16)],
        compiler_params=_params(2),
        name="ffn",
    )(x, g_pre, wg, wu, wd, g_post)


def _norm_matmul_kernel(x_ref, g_ref, w_ref, o_ref, xn_ref):
    @pl.when(pl.program_id(1) == 0)
    def _():
        xn_ref[...] = _rms(x_ref[...], g_ref[...]).astype(BF16)

    o_ref[...] = _dot(xn_ref[...], w_ref[...])


def _norm_matmul(x, g, w, layer, *, tm, tn):
    n, d = x.shape
    m = w.shape[-1]
    tm, tn = min(tm, n), min(tn, m)
    assert n % tm == 0 and m % tn == 0
    return pl.pallas_call(
        _norm_matmul_kernel,
        grid=(n // tm, m // tn),
        in_specs=[
            pl.BlockSpec((tm, d), lambda i, j: (i, 0)),
            pl.BlockSpec((None, 1, d), lambda i, j: (layer, 0, 0)),
            pl.BlockSpec((None, d, tn), lambda i, j: (layer, 0, j)),
        ],
        out_specs=pl.BlockSpec((tm, tn), lambda i, j: (i, j)),
        out_shape=jax.ShapeDtypeStruct((n, m), F32),
        scratch_shapes=[pltpu.VMEM((tm, d), BF16)],
        compiler_params=_params(2),
        name="norm_matmul",
    )(x, g, w)


def _mem_kv_kernel(x_ref, g_ref, wk_ref, wv_ref, k_ref, v_ref):
    xn = _rms(x_ref[...], g_ref[...]).astype(BF16)
    k_ref[...] = _dot(xn, wk_ref[...])
    v_ref[...] = _dot(xn, wv_ref[...])


def _mem_kv(mem, g, wk, wv, *, tm):
    r, d = mem.shape
    n_layers = wk.shape[0]
    tm = min(tm, r)
    assert r % tm == 0
    out = jax.ShapeDtypeStruct((n_layers, r, MEM_WIDTH), F32)
    return pl.pallas_call(
        _mem_kv_kernel,
        grid=(n_layers, r // tm),
        in_specs=[
            pl.BlockSpec((tm, d), lambda l, i: (i, 0)),
            pl.BlockSpec((None, 1, d), lambda l, i: (l, 0, 0)),
            pl.BlockSpec((None, d, MEM_WIDTH), lambda l, i: (l, 0, 0)),
            pl.BlockSpec((None, d, MEM_WIDTH), lambda l, i: (l, 0, 0)),
        ],
        out_specs=[pl.BlockSpec((None, tm, MEM_WIDTH), lambda l, i: (l, i, 0))] * 2,
        out_shape=[out, out],
        compiler_params=_params(2),
        name="mem_kv",
    )(mem, g, wk, wv)


def _pool_conv_kernel(zp_ref, cb_ref, cc_ref, cx_ref, hp_ref, hc_ref, pw_ref, ps_ref, cw_ref,
                      yp_ref, yc_ref, nc_ref, extp_ref, extc_ref, *, seq, pos0, rows):
    extp_ref[0:POOL_HIST_PAD, :] = hp_ref[...]
    extp_ref[POOL_HIST_PAD:POOL_HIST_PAD + seq, :] = zp_ref[...]
    extc_ref[0:CONV_HIST_PAD, :] = hc_ref[...]
    extc_ref[CONV_HIST_PAD:CONV_HIST_PAD + seq, :] = cc_ref[...] * cx_ref[...]
    nc_ref[...] = extc_ref[CONV_HIST_PAD + seq - CONV_HIST:CONV_HIST_PAD + seq, :]

    for r0 in range(0, seq, rows):
        pos = pos0 + r0 + lax.broadcasted_iota(jnp.int32, (rows, 1), 0)
        for g, win in enumerate(POOL_WINDOWS):
            c = slice(g * POOL_GROUP_DIM, (g + 1) * POOL_GROUP_DIM)
            base = POOL_HIST_PAD + r0
            tok = extp_ref[base:base + rows, c]
            wsum = tok
            for k in range(1, win):
                wsum = wsum + extp_ref[base - k:base - k + rows, c]
            cnt = jnp.minimum(pos + 1, win).astype(F32)
            pooled = wsum / cnt - tok
            mixed = _dot(pooled.astype(BF16), pw_ref[g])
            yp_ref[r0:r0 + rows, c] = (mixed * ps_ref[:, c]).astype(BF16)
        base = CONV_HIST_PAD + r0
        y = cw_ref[0:1, :] * extc_ref[base - 2:base - 2 + rows, :]
        y = y + cw_ref[1:2, :] * extc_ref[base - 1:base - 1 + rows, :]
        y = y + cw_ref[2:3, :] * extc_ref[base:base + rows, :]
        yc_ref[r0:r0 + rows, :] = (cb_ref[r0:r0 + rows, :] * y).astype(BF16)


def _pool_conv(z3, hist_pool, hist_conv, pool_w, pool_scale, conv_w, layer, *, pos0):
    b, seq, _ = z3.shape
    assert seq >= CONV_HIST and seq % 8 == 0
    rows = min(seq, 256)
    assert seq % rows == 0
    col = lambda off: (lambda i: (i, 0, off // 512))
    seq_block = lambda off: pl.BlockSpec((None, seq, 512), col(off))
    return pl.pallas_call(
        functools.partial(_pool_conv_kernel, seq=seq, pos0=pos0, rows=rows),
        grid=(b,),
        in_specs=[
            seq_block(0), seq_block(OFF_CB), seq_block(OFF_CC), seq_block(OFF_CX),
            pl.BlockSpec((None, POOL_HIST_PAD, POOL_WIDTH), lambda i: (i, 0, 0)),
            pl.BlockSpec((None, CONV_HIST_PAD, CONV_WIDTH), lambda i: (i, 0, 0)),
            pl.BlockSpec((None, len(POOL_WINDOWS), POOL_GROUP_DIM, POOL_GROUP_DIM),
                         lambda i: (layer, 0, 0, 0)),
            pl.BlockSpec((None, 1, POOL_WIDTH), lambda i: (layer, 0, 0)),
            pl.BlockSpec((None, CONV_K, CONV_WIDTH), lambda i: (layer, 0, 0)),
        ],
        out_specs=[
            pl.BlockSpec((None, seq, POOL_WIDTH), lambda i: (i, 0, 0)),
            pl.BlockSpec((None, seq, CONV_WIDTH), lambda i: (i, 0, 0)),
            pl.BlockSpec((None, CONV_HIST, CONV_WIDTH), lambda i: (i, 0, 0)),
        ],
        out_shape=[
            jax.ShapeDtypeStruct((b, seq, POOL_WIDTH), BF16),
            jax.ShapeDtypeStruct((b, seq, CONV_WIDTH), BF16),
            jax.ShapeDtypeStruct((b, CONV_HIST, CONV_WIDTH), F32),
        ],
        scratch_shapes=[pltpu.VMEM((POOL_HIST_PAD + seq, POOL_WIDTH), F32),
                        pltpu.VMEM((CONV_HIST_PAD + seq, CONV_WIDTH), F32)],
        compiler_params=_params(1),
        name="pool_conv",
    )(z3, z3, z3, z3, hist_pool, hist_conv, pool_w, pool_scale, conv_w)


def _diff_lambda(lq1_ref, lk1_ref, lq2_ref, lk2_ref, lam_init):
    a = jnp.sum(lq1_ref[...] * lk1_ref[...], axis=-1, keepdims=True)
    b = jnp.sum(lq2_ref[...] * lk2_ref[...], axis=-1, keepdims=True)
    return jnp.exp(a) - jnp.exp(b) + lam_init


def _diff_prompt_kernel(slopes_ref, q_ref, k_ref, v_ref, lq1_ref, lk1_ref, lq2_ref, lk2_ref, g_ref,
                        o_ref, kb_ref, vb_ref, bm_ref, mx_ref, l_ref, acc_ref, *, tq, tk, lam_init):
    h = pl.program_id(1)
    qi = pl.program_id(2)
    n_diag = tq // tk

    @pl.when((pl.program_id(0) == 0) & (h == 0) & (qi == 0))
    def _():
        ii = lax.broadcasted_iota(jnp.int32, (tq, tk), 0)
        jj = lax.broadcasted_iota(jnp.int32, (tq, tk), 1)
        for d in range(n_diag):
            kj = jj + d * tk
            base = (ii - jnp.abs(ii - kj)).astype(F32)
            bm_ref[d] = jnp.where(kj <= (ii | (CHUNK - 1)), base, -jnp.inf)

    @pl.when(qi == 0)
    def _():
        kb_ref[...] = k_ref[...].astype(BF16)
        vb_ref[...] = v_ref[...].astype(BF16)

    q = q_ref[...]
    qs = (q[:, :DIFF_HEAD_DIM].astype(BF16), q[:, DIFF_HEAD_DIM:].astype(BF16))
    scale = DIFF_HEAD_DIM ** -0.5
    slope = slopes_ref[h]
    col = lax.broadcasted_iota(jnp.int32, (1, tk), 1)

    def key_start(kt, d):
        return pl.multiple_of((kt if d is None else qi * n_diag + d) * tk, tk)

    def scores(kt, c, d):
        k0 = key_start(kt, d)
        kk = kb_ref[pl.ds(k0, tk), c * DIFF_HEAD_DIM:(c + 1) * DIFF_HEAD_DIM]
        s = _dot_nt(qs[c], kk) * scale
        if d is None:
            return s + slope * (k0 - qi * tq + col).astype(F32)
        return s + slope * bm_ref[d]

    def earlier_tiles(fn):
        def body(i, carry):
            for u in range(n_diag):
                fn(i * n_diag + u, None)
            return carry
        lax.fori_loop(0, qi, body, 0)

    def pass1(kt, d):
        for c in range(2):
            s = scores(kt, c, d)
            mx_ref[c] = s if d == 0 else jnp.maximum(mx_ref[c], s)

    for d in range(n_diag):
        pass1(None, d)
    earlier_tiles(pass1)
    m_full = [jnp.broadcast_to(jnp.max(mx_ref[c], axis=-1, keepdims=True), (tq, tk)) for c in range(2)]

    def pass2(kt, d):
        vv = vb_ref[pl.ds(key_start(kt, d), tk), :]
        for c in range(2):
            p = jnp.exp(scores(kt, c, d) - m_full[c])
            pv = _dot(p.astype(BF16), vv)
            if d == 0:
                l_ref[c] = p
                acc_ref[c] = pv
            else:
                l_ref[c] += p
                acc_ref[c] += pv

    for d in range(n_diag):
        pass2(None, d)
    earlier_tiles(pass2)

    lam = _diff_lambda(lq1_ref, lk1_ref, lq2_ref, lk2_ref, lam_init)
    l1 = jnp.sum(l_ref[0], axis=-1, keepdims=True)
    l2 = jnp.sum(l_ref[1], axis=-1, keepdims=True)
    o = acc_ref[0] / l1 - lam * (acc_ref[1] / l2)
    o_ref[...] = (_rms(o, g_ref[...]) * (1.0 - lam_init)).astype(BF16)


def _diff_lambda_specs(layer, n_axes):
    idx = {2: lambda a, b: (layer, 0, 0), 3: lambda a, b, c: (layer, 0, 0)}[n_axes]
    return [pl.BlockSpec((None, 1, DIFF_HEAD_DIM), idx)] * 4 + [pl.BlockSpec((None, 1, DIFF_V_DIM), idx)]


def _diff_attn_prompt(z3, slopes, lq1, lk1, lq2, lk2, subln_g, layer, *, tq, tk, lam_init):
    b, seq, _ = z3.shape
    tq = min(tq, seq)
    assert seq % tq == 0 and tq % tk == 0 and tk % CHUNK == 0 and CHUNK & (CHUNK - 1) == 0
    col = lambda off: off // DIFF_V_DIM
    return pl.pallas_call(
        functools.partial(_diff_prompt_kernel, tq=tq, tk=tk, lam_init=lam_init),
        grid=(b, N_DIFF_HEADS, seq // tq),
        in_specs=[
            pl.BlockSpec(memory_space=pltpu.SMEM),
            pl.BlockSpec((None, tq, DIFF_V_DIM), lambda i, h, q: (i, q, col(OFF_Q) + h)),
            pl.BlockSpec((None, seq, DIFF_V_DIM), lambda i, h, q: (i, 0, col(OFF_K) + h)),
            pl.BlockSpec((None, seq, DIFF_V_DIM), lambda i, h, q: (i, 0, col(OFF_V) + h)),
        ] + _diff_lambda_specs(layer, 3),
        out_specs=pl.BlockSpec((None, tq, DIFF_V_DIM), lambda i, h, q: (i, q, h)),
        out_shape=jax.ShapeDtypeStruct((b, seq, DIFF_WIDTH), BF16),
        scratch_shapes=[
            pltpu.VMEM((seq, DIFF_V_DIM), BF16),
            pltpu.VMEM((seq, DIFF_V_DIM), BF16),
            pltpu.VMEM((tq // tk, tq, tk), F32),
            pltpu.VMEM((2, tq, tk), F32),
            pltpu.VMEM((2, tq, tk), F32),
            pltpu.VMEM((2, tq, DIFF_V_DIM), F32),
        ],
        compiler_params=_params(3),
        name="diff_attn_prompt",
    )(slopes, z3, z3, z3, lq1, lk1, lq2, lk2, subln_g)


def _diff_sample_kernel(slopes_ref, q_ref, kn_ref, vn_ref, kc_ref, vc_ref,
                        lq1_ref, lk1_ref, lq2_ref, lk2_ref, g_ref,
                        o_ref, m_ref, l_ref, acc_ref, *, past, seq, tk, lam_init):
    kt = pl.program_id(1)
    n_cache_tiles = past // tk

    @pl.when(kt == 0)
    def _():
        m_ref[...] = jnp.full(m_ref.shape, -jnp.inf, F32)
        l_ref[...] = jnp.zeros(l_ref.shape, F32)
        acc_ref[...] = jnp.zeros(acc_ref.shape, F32)

    scale = DIFF_HEAD_DIM ** -0.5

    def process(k_heads, v_heads, n_keys, new_keys):
        ii = lax.broadcasted_iota(jnp.int32, (seq, n_keys), 0)
        jj = lax.broadcasted_iota(jnp.int32, (seq, n_keys), 1)
        if new_keys:
            dist = jnp.abs(ii - jj)
        else:
            dist = ii - jj + (past - kt * tk)
        dist = dist.astype(F32)
        s_all = []
        for h in range(N_DIFF_HEADS):
            bias = -slopes_ref[h] * dist
            for c in range(2):
                lo = h * DIFF_V_DIM + c * DIFF_HEAD_DIM
                qc = q_ref[:, lo:lo + DIFF_HEAD_DIM].astype(BF16)
                kc = k_heads[h][:, c * DIFF_HEAD_DIM:(c + 1) * DIFF_HEAD_DIM]
                s_all.append(_dot_nt(qc, kc) * scale + bias)
        s = jnp.stack(s_all)
        m_old = m_ref[...]
        m_new = jnp.maximum(m_old, jnp.max(s, axis=-1, keepdims=True))
        alpha = jnp.exp(m_old - m_new)
        p = jnp.exp(s - m_new)
        l_ref[...] = alpha * l_ref[...] + jnp.sum(p, axis=-1, keepdims=True)
        pb = p.astype(BF16)
        pv = jnp.stack([_dot(pb[i], v_heads[i // 2]) for i in range(2 * N_DIFF_HEADS)])
        acc_ref[...] = alpha * acc_ref[...] + pv
        m_ref[...] = m_new

    @pl.when(kt < n_cache_tiles)
    def _():
        kk = jnp.swapaxes(kc_ref[...], 0, 1).astype(BF16)
        vv = jnp.swapaxes(vc_ref[...], 0, 1).astype(BF16)
        process([kk[h] for h in range(N_DIFF_HEADS)], [vv[h] for h in range(N_DIFF_HEADS)], tk, False)

    @pl.when(kt == n_cache_tiles)
    def _():
        head = lambda ref, h: ref[:, h * DIFF_V_DIM:(h + 1) * DIFF_V_DIM].astype(BF16)
        process([head(kn_ref, h) for h in range(N_DIFF_HEADS)],
                [head(vn_ref, h) for h in range(N_DIFF_HEADS)], seq, True)
        lam = _diff_lambda(lq1_ref, lk1_ref, lq2_ref, lk2_ref, lam_init)
        for h in range(N_DIFF_HEADS):
            o = acc_ref[2 * h] / l_ref[2 * h] - lam * (acc_ref[2 * h + 1] / l_ref[2 * h + 1])
            y = _rms(o, g_ref[...]) * (1.0 - lam_init)
            o_ref[:, h * DIFF_V_DIM:(h + 1) * DIFF_V_DIM] = y.astype(BF16)


def _diff_attn_sample(q3, k3, v3, cache_k, cache_v, slopes, lq1, lk1, lq2, lk2, subln_g, layer, *, tk, lam_init):
    b, seq, _ = q3.shape
    past = cache_k.shape[2]
    assert past % tk == 0 and past % CHUNK == 0 and seq <= CHUNK
    n_cache_tiles = past // tk
    cache_spec = pl.BlockSpec((None, None, tk, N_DIFF_HEADS, DIFF_V_DIM),
                              lambda i, k: (layer, i, jnp.minimum(k, n_cache_tiles - 1), 0, 0))
    new_spec = pl.BlockSpec((None, seq, DIFF_WIDTH), lambda i, k: (i, 0, 0))
    return pl.pallas_call(
        functools.partial(_diff_sample_kernel, past=past, seq=seq, tk=tk, lam_init=lam_init),
        grid=(b, n_cache_tiles + 1),
        in_specs=[
            pl.BlockSpec(memory_space=pltpu.SMEM),
            new_spec, new_spec, new_spec,
            cache_spec, cache_spec,
        ] + _diff_lambda_specs(layer, 2),
        out_specs=pl.BlockSpec((None, seq, DIFF_WIDTH), lambda i, k: (i, 0, 0)),
        out_shape=jax.ShapeDtypeStruct((b, seq, DIFF_WIDTH), BF16),
        scratch_shapes=[
            pltpu.VMEM((2 * N_DIFF_HEADS, seq, 1), F32),
            pltpu.VMEM((2 * N_DIFF_HEADS, seq, 1), F32),
            pltpu.VMEM((2 * N_DIFF_HEADS, seq, DIFF_V_DIM), F32),
        ],
        compiler_params=_params(2),
        name="diff_attn_sample",
    )(slopes, q3, k3, v3, cache_k, cache_v, lq1, lk1, lq2, lk2, subln_g)


def _out_proj_kernel(yp_ref, yd_ref, yc_ref, w_ref, x_ref, g_ref, o_ref):
    m = _dot(yp_ref[...], w_ref[0:POOL_WIDTH, :])
    m = m + _dot(yd_ref[...], w_ref[POOL_WIDTH:POOL_WIDTH + DIFF_WIDTH, :])
    m = m + _dot(yc_ref[...], w_ref[POOL_WIDTH + DIFF_WIDTH:, :])
    o_ref[...] = x_ref[...] + _rms(m, g_ref[...])


def _out_proj(yp, yd, yc, w_out, x, g_post, layer, *, tm):
    n, d = x.shape
    tm = min(tm, n)
    assert n % tm == 0
    row = lambda width: pl.BlockSpec((tm, width), lambda i: (i, 0))
    return pl.pallas_call(
        _out_proj_kernel,
        grid=(n // tm,),
        in_specs=[
            row(POOL_WIDTH), row(DIFF_WIDTH), row(CONV_WIDTH),
            pl.BlockSpec((None, w_out.shape[1], d), lambda i: (layer, 0, 0)),
            row(d),
            pl.BlockSpec((None, 1, d), lambda i: (layer, 0, 0)),
        ],
        out_specs=row(d),
        out_shape=jax.ShapeDtypeStruct((n, d), F32),
        compiler_params=_params(1),
        name="out_proj",
    )(yp, yd, yc, w_out, x, g_post)


def _mem_attn_kernel(x_ref, gpre_ref, wq_ref, mk_ref, mv_ref, wo_ref, gpost_ref, o_ref):
    x = x_ref[...]
    q = _dot(_rms(x, gpre_ref[...]).astype(BF16), wq_ref[...])
    scale = MEM_HEAD_DIM ** -0.5
    heads = []
    for h in range(N_MEM_HEADS):
        c = slice(h * MEM_HEAD_DIM, (h + 1) * MEM_HEAD_DIM)
        s = _dot_nt(q[:, c].astype(BF16), mk_ref[:, c].astype(BF16)) * scale
        p = jnp.exp(s - jnp.max(s, axis=-1, keepdims=True))
        p = p / jnp.sum(p, axis=-1, keepdims=True)
        heads.append(_dot(p.astype(BF16), mv_ref[:, c].astype(BF16)))
    o = jnp.concatenate(heads, axis=-1).astype(BF16)
    o_ref[...] = x + _rms(_dot(o, wo_ref[...]), gpost_ref[...])


def _mem_attn(x3, g_pre, w_mq, mk, mv, w_mo, g_post, layer, mem_layer, *, tm):
    b, seq, d = x3.shape
    mem_len = mk.shape[2]
    tm = min(tm, seq)
    assert seq % tm == 0
    mem_spec = pl.BlockSpec((None, None, mem_len, MEM_WIDTH), lambda i, r: (mem_layer, i, 0, 0))
    gain = pl.BlockSpec((None, 1, d), lambda i, r: (layer, 0, 0))
    return pl.pallas_call(
        _mem_attn_kernel,
        grid=(b, seq // tm),
        in_specs=[
            pl.BlockSpec((None, tm, d), lambda i, r: (i, r, 0)),
            gain,
            pl.BlockSpec((None, d, MEM_WIDTH), lambda i, r: (layer, 0, 0)),
            mem_spec, mem_spec,
            pl.BlockSpec((None, MEM_WIDTH, d), lambda i, r: (layer, 0, 0)),
            gain,
        ],
        out_specs=pl.BlockSpec((None, tm, d), lambda i, r: (i, r, 0)),
        out_shape=jax.ShapeDtypeStruct((b, seq, d), F32),
        compiler_params=_params(2),
        name="mem_attn",
    )(x3, g_pre, w_mq, mk, mv, w_mo, g_post)


def _pad_front(hist, rows):
    return jnp.pad(hist, ((0, 0), (rows - hist.shape[1], 0), (0, 0)))


def kernel(x_prompt, x_sample, mem_prompt, cache_diff_k, cache_diff_v, state_pool, state_conv, cache_mem_k, cache_mem_v, ffn1_pre_g, ffn1_post_g, ffn1_wg, ffn1_wu, ffn1_wd, mix_pre_g, mix_post_g, w_in, pool_w, pool_scale, diff_lq1, diff_lk1, diff_lq2, diff_lk2, diff_subln_g, conv_w, w_out, mem_pre_g, mem_post_g, mem_kv_g, w_mq, w_mk, w_mv, w_mo, ffn2_pre_g, ffn2_post_g, ffn2_wg, ffn2_wu, ffn2_wd):
    n_layers = w_in.shape[0]
    bp, sp, d = x_prompt.shape
    bs, ss, _ = x_sample.shape
    past = cache_diff_k.shape[2]
    mem_len = mem_prompt.shape[1]
    assert sp >= POOL_HIST and ss >= POOL_HIST

    bf = lambda w: w.astype(BF16)
    row = lambda v: v.reshape(n_layers, 1, v.shape[-1])
    ffn1 = (row(ffn1_pre_g), bf(ffn1_wg), bf(ffn1_wu), bf(ffn1_wd), row(ffn1_post_g))
    ffn2 = (row(ffn2_pre_g), bf(ffn2_wg), bf(ffn2_wu), bf(ffn2_wd), row(ffn2_post_g))
    mix_pre_g, mix_post_g = row(mix_pre_g), row(mix_post_g)
    mem_pre_g, mem_post_g, mem_kv_g = row(mem_pre_g), row(mem_post_g), row(mem_kv_g)
    w_in, w_out, w_mq, w_mk, w_mv, w_mo, pool_w = map(bf, (w_in, w_out, w_mq, w_mk, w_mv, w_mo, pool_w))
    pool_scale = row(pool_scale)
    lam_vecs = tuple(row(v) for v in (diff_lq1, diff_lk1, diff_lq2, diff_lk2, diff_subln_g))
    slopes = jnp.asarray([2.0 ** (-8.0 * (h + 1) / N_DIFF_HEADS) for h in range(N_DIFF_HEADS)], F32)

    mk_p, mv_p = _mem_kv(mem_prompt.reshape(bp * mem_len, d), mem_kv_g, w_mk, w_mv, tm=1024)
    mk_p = mk_p.reshape(n_layers, bp, mem_len, MEM_WIDTH)
    mv_p = mv_p.reshape(n_layers, bp, mem_len, MEM_WIDTH)
    mk_s = cache_mem_k.reshape(n_layers, bs, mem_len, MEM_WIDTH)
    mv_s = cache_mem_v.reshape(n_layers, bs, mem_len, MEM_WIDTH)

    zero_pool = jnp.zeros((bp, POOL_HIST_PAD, POOL_WIDTH), F32)
    zero_conv = jnp.zeros((bp, CONV_HIST_PAD, CONV_WIDTH), F32)

    xp = x_prompt.reshape(bp * sp, d)
    xs = x_sample.reshape(bs * ss, d)
    outs = {name: [] for name in ("kp", "vp", "pp", "cp", "ks", "vs", "ps", "cs")}

    def layer_step(x, b, seq, l, hist_pool, hist_conv, pos0, diff_attn, mk, mv, mem_tm):
        lam_init = 0.8 - 0.6 * math.exp(-0.3 * l)
        x = _ffn(x, *ffn1, l, tm=512, tf=512)
        z = _norm_matmul(x, mix_pre_g, w_in, l, tm=1024, tn=512)
        z3 = z.reshape(b, seq, z.shape[-1])
        y_pool, y_conv, new_conv = _pool_conv(z3, hist_pool, hist_conv, pool_w, pool_scale, conv_w, l, pos0=pos0)
        y_diff = diff_attn(z3, l, lam_init)
        x = _out_proj(y_pool.reshape(b * seq, -1), y_diff.reshape(b * seq, -1), y_conv.reshape(b * seq, -1),
                      w_out, x, mix_post_g, l, tm=512)
        x = _mem_attn(x.reshape(b, seq, d), mem_pre_g, w_mq, mk, mv, w_mo, mem_post_g, l, l, tm=mem_tm)
        x = _ffn(x.reshape(b * seq, d), *ffn2, l, tm=512, tf=512)
        k_rows = z3[:, :, OFF_K:OFF_V].reshape(b, seq, N_DIFF_HEADS, DIFF_V_DIM)
        v_rows = z3[:, :, OFF_V:OFF_CB].reshape(b, seq, N_DIFF_HEADS, DIFF_V_DIM)
        new_pool = z3[:, seq - POOL_HIST:, :POOL_WIDTH]
        return x, k_rows, v_rows, new_pool, new_conv

    def prompt_attn(z3, l, lam_init):
        return _diff_attn_prompt(z3, slopes, *lam_vecs, l, tq=512, tk=256, lam_init=lam_init)

    def sample_attn(z3, l, lam_init):
        q3, k3, v3 = (z3[:, :, off:off + DIFF_WIDTH] for off in (OFF_Q, OFF_K, OFF_V))
        return _diff_attn_sample(q3, k3, v3, cache_diff_k, cache_diff_v, slopes, *lam_vecs, l,
                                 tk=min(1024, past), lam_init=lam_init)

    for l in range(n_layers):
        xp, kp, vp, pp, cp = layer_step(xp, bp, sp, l, zero_pool, zero_conv, 0, prompt_attn, mk_p, mv_p, 512)
        xs, ks, vs, ps, cs = layer_step(xs, bs, ss, l, _pad_front(state_pool[l], POOL_HIST_PAD),
                                        _pad_front(state_conv[l], CONV_HIST_PAD), past, sample_attn,
                                        mk_s, mv_s, ss)
        for name, val in zip(("kp", "vp", "pp", "cp", "ks", "vs", "ps", "cs"), (kp, vp, pp, cp, ks, vs, ps, cs)):
            outs[name].append(val)

    st = lambda name: jnp.stack(outs[name])
    head_shape = (n_layers, bp, mem_len, N_MEM_HEADS, MEM_HEAD_DIM)
    return (xp.reshape(bp, sp, d), xs.reshape(bs, ss, d),
            st("kp"), st("vp"), st("pp"), st("cp"),
            mk_p.reshape(head_shape), mv_p.reshape(head_shape),
            st("ks"), st("vs"), st("ps"), st("cs"))
```

```python
import functools
import math

import jax
import jax.numpy as jnp
from jax import lax
from jax.experimental import pallas as pl
from jax.experimental.pallas import tpu as pltpu

F32 = jnp.float32
BF16 = jnp.bfloat16

EPS = 1e-6
CHUNK = 64
POOL_WINDOWS = (2, 4, 8, 16)
POOL_GROUP_DIM = 128
POOL_WIDTH = POOL_GROUP_DIM * len(POOL_WINDOWS)
POOL_HIST = max(POOL_WINDOWS) - 1
POOL_HIST_PAD = 16
N_DIFF_HEADS = 4
DIFF_HEAD_DIM = 128
DIFF_V_DIM = 2 * DIFF_HEAD_DIM
DIFF_WIDTH = N_DIFF_HEADS * DIFF_V_DIM
CONV_WIDTH = 512
CONV_K = 3
CONV_HIST = CONV_K - 1
CONV_HIST_PAD = 8
N_MEM_HEADS = 4
MEM_HEAD_DIM = 128
MEM_WIDTH = N_MEM_HEADS * MEM_HEAD_DIM
OFF_Q = POOL_WIDTH
OFF_K = OFF_Q + DIFF_WIDTH
OFF_V = OFF_K + DIFF_WIDTH
OFF_CB = OFF_V + DIFF_WIDTH
OFF_CC = OFF_CB + CONV_WIDTH
OFF_CX = OFF_CC + CONV_WIDTH

V7X_VMEM_LIMIT_BYTES = 56 * 1024 * 1024


def _params(n_axes):
    return pltpu.CompilerParams(dimension_semantics=("arbitrary",) * n_axes,
                                vmem_limit_bytes=V7X_VMEM_LIMIT_BYTES)


def _rms(xf, g):
    ms = jnp.mean(xf * xf, axis=-1, keepdims=True)
    return xf * lax.rsqrt(ms + EPS) * g


def _dot(a, b):
    return jnp.dot(a, b, preferred_element_type=F32)


def _dot_nt(a, b):
    return lax.dot_general(a, b, (((1,), (1,)), ((), ())), preferred_element_type=F32)


def _ffn_kernel(x_ref, gpre_ref, wg_ref, wu_ref, wd_ref, gpost_ref, y_ref, xn_ref, *, n_ff_steps):
    j = pl.program_id(1)

    @pl.when(j == 0)
    def _():
        xn_ref[...] = _rms(x_ref[...], gpre_ref[...]).astype(BF16)
        y_ref[...] = jnp.zeros(y_ref.shape, F32)

    xn = xn_ref[...]
    gate = _dot(xn, wg_ref[...])
    up = _dot(xn, wu_ref[...])
    h = (gate * jax.nn.sigmoid(gate)) * up
    y_ref[...] += _dot(h.astype(BF16), wd_ref[...])

    @pl.when(j == n_ff_steps - 1)
    def _():
        y_ref[...] = x_ref[...] + 0.5 * _rms(y_ref[...], gpost_ref[...])


def _ffn(x, g_pre, wg, wu, wd, g_post, layer, *, tm, tf):
    n, d = x.shape
    f = wg.shape[-1]
    tm, tf = min(tm, n), min(tf, f)
    assert n % tm == 0 and f % tf == 0
    return pl.pallas_call(
        functools.partial(_ffn_kernel, n_ff_steps=f // tf),
        grid=(n // tm, f // tf),
        in_specs=[
            pl.BlockSpec((tm, d), lambda i, j: (i, 0)),
            pl.BlockSpec((None, 1, d), lambda i, j: (layer, 0, 0)),
            pl.BlockSpec((None, d, tf), lambda i, j: (layer, 0, j)),
            pl.BlockSpec((None, d, tf), lambda i, j: (layer, 0, j)),
            pl.BlockSpec((None, tf, d), lambda i, j: (layer, j, 0)),
            pl.BlockSpec((None, 1, d), lambda i, j: (layer, 0, 0)),
        ],
        out_specs=pl.BlockSpec((tm, d), lambda i, j: (i, 0)),
        out_shape=jax.ShapeDtypeStruct((n, d), F32),
        scratch_shapes=[pltpu.VMEM((tm, d), BF16)],
        compiler_params=_params(2),
        name="ffn",
    )(x, g_pre, wg, wu, wd, g_post)


IN_PROJ_BLOCK = 1024
IN_PROJ_ORDER = ((OFF_Q, OFF_K), (OFF_K, OFF_V), (OFF_V, OFF_CB), (0, OFF_Q), (OFF_CB, OFF_CC), (OFF_CC, OFF_CX),
                 (OFF_CX, OFF_CX + CONV_WIDTH))


def _permute_in_proj(w_in):
    return jnp.concatenate([w_in[..., lo:hi] for lo, hi in IN_PROJ_ORDER], axis=-1)


def _store_heads(z_ref, rows_ref, heads_ref):
    heads = jnp.stack([z_ref[:, h * DIFF_V_DIM:(h + 1) * DIFF_V_DIM] for h in range(N_DIFF_HEADS)])
    rows_ref[...] = jnp.swapaxes(heads, 0, 1)
    if heads_ref is not None:
        heads_ref[...] = heads.astype(BF16)


def _in_proj_kernel(*refs, head_major, aliased):
    refs = list(refs)
    x_ref, g_ref, w_ref = refs[:3]
    refs = refs[3 + (2 if aliased else 0):]
    q_ref, kf_ref, vf_ref = refs[:3]
    refs = refs[3:]
    kb_ref, vb_ref = (refs[0], refs[1]) if head_major else (None, None)
    refs = refs[2 if head_major else 0:]
    zp_ref, cb_ref, cc_ref, cx_ref, xn_ref, zk_ref, zv_ref = refs
    j = pl.program_id(1)
    half = IN_PROJ_BLOCK // 2

    @pl.when(j == 0)
    def _():
        xn_ref[...] = _rms(x_ref[...], g_ref[...]).astype(BF16)
        q_ref[...] = _dot(xn_ref[...], w_ref[...]).astype(BF16)

    @pl.when(j == 1)
    def _():
        zk_ref[...] = _dot(xn_ref[...], w_ref[...])

    @pl.when(j == 2)
    def _():
        zv_ref[...] = _dot(xn_ref[...], w_ref[...])
        _store_heads(zk_ref, kf_ref, kb_ref)

    @pl.when(j == 3)
    def _():
        z = _dot(xn_ref[...], w_ref[...])
        zp_ref[...] = z[:, :half]
        cb_ref[...] = z[:, half:]
        _store_heads(zv_ref, vf_ref, vb_ref)

    @pl.when(j == 4)
    def _():
        z = _dot(xn_ref[...], w_ref[...])
        cc_ref[...] = z[:, :half]
        cx_ref[...] = z[:, half:]


def _in_proj(x, g, w_perm, layer, k_stack, v_stack, *, n_layers, seq, tm, head_major):
    n, d = x.shape
    tm = min(tm, n, seq) if head_major else min(tm, n)
    assert n % tm == 0 and w_perm.shape[-1] == 5 * IN_PROJ_BLOCK and (not head_major or seq % tm == 0)
    aliased = k_stack is not None
    rows = lambda width, dt: (pl.BlockSpec((tm, width), lambda i, j: (i, 0)), jax.ShapeDtypeStruct((n, width), dt))
    stack = (pl.BlockSpec((None, tm, N_DIFF_HEADS, DIFF_V_DIM), lambda i, j: (layer, i, 0, 0)),
             jax.ShapeDtypeStruct((n_layers, n, N_DIFF_HEADS, DIFF_V_DIM), F32))
    tiles = seq // tm if head_major else 1
    heads = (pl.BlockSpec((None, N_DIFF_HEADS, tm, DIFF_V_DIM), lambda i, j: (i // tiles, 0, i % tiles, 0)),
             jax.ShapeDtypeStruct((n // seq, N_DIFF_HEADS, seq, DIFF_V_DIM), BF16))
    outs = [rows(DIFF_WIDTH, BF16), stack, stack] + ([heads, heads] if head_major else []) + [rows(512, F32)] * 4
    in_specs = [
        pl.BlockSpec((tm, d), lambda i, j: (i, 0)),
        pl.BlockSpec((None, 1, d), lambda i, j: (layer, 0, 0)),
        pl.BlockSpec((None, d, IN_PROJ_BLOCK), lambda i, j: (layer, 0, j)),
    ]
    args = [x, g, w_perm]
    if aliased:
        in_specs += [pl.BlockSpec(memory_space=pl.ANY)] * 2
        args += [k_stack, v_stack]
    return pl.pallas_call(
        functools.partial(_in_proj_kernel, head_major=head_major, aliased=aliased),
        grid=(n // tm, 5),
        in_specs=in_specs,
        out_specs=[o[0] for o in outs],
        out_shape=[o[1] for o in outs],
        input_output_aliases={3: 1, 4: 2} if aliased else {},
        scratch_shapes=[pltpu.VMEM((tm, d), BF16), pltpu.VMEM((tm, DIFF_WIDTH), F32), pltpu.VMEM((tm, DIFF_WIDTH), F32)],
        compiler_params=_params(2),
        name="in_proj",
    )(*args)


def _mem_kv_kernel(x_ref, g_ref, wk_ref, wv_ref, k_ref, v_ref):
    xn = _rms(x_ref[...], g_ref[...]).astype(BF16)
    k_ref[...] = _dot(xn, wk_ref[...])
    v_ref[...] = _dot(xn, wv_ref[...])


def _mem_kv(mem, g, wk, wv, *, tm):
    r, d = mem.shape
    n_layers = wk.shape[0]
    tm = min(tm, r)
    assert r % tm == 0
    out = jax.ShapeDtypeStruct((n_layers, r, MEM_WIDTH), F32)
    return pl.pallas_call(
        _mem_kv_kernel,
        grid=(n_layers, r // tm),
        in_specs=[
            pl.BlockSpec((tm, d), lambda l, i: (i, 0)),
            pl.BlockSpec((None, 1, d), lambda l, i: (l, 0, 0)),
            pl.BlockSpec((None, d, MEM_WIDTH), lambda l, i: (l, 0, 0)),
            pl.BlockSpec((None, d, MEM_WIDTH), lambda l, i: (l, 0, 0)),
        ],
        out_specs=[pl.BlockSpec((None, tm, MEM_WIDTH), lambda l, i: (l, i, 0))] * 2,
        out_shape=[out, out],
        compiler_params=_params(2),
        name="mem_kv",
    )(mem, g, wk, wv)


def _pool_conv_kernel(zp_ref, cb_ref, cc_ref, cx_ref, hp_ref, hc_ref, pw_ref, ps_ref, cw_ref,
                      yp_ref, yc_ref, nc_ref, extp_ref, extc_ref, *, seq, pos0, rows):
    extp_ref[0:POOL_HIST_PAD, :] = hp_ref[...]
    extp_ref[POOL_HIST_PAD:POOL_HIST_PAD + seq, :] = zp_ref[...]
    extc_ref[0:CONV_HIST_PAD, :] = hc_ref[...]
    extc_ref[CONV_HIST_PAD:CONV_HIST_PAD + seq, :] = cc_ref[...] * cx_ref[...]
    nc_ref[...] = extc_ref[CONV_HIST_PAD + seq - CONV_HIST:CONV_HIST_PAD + seq, :]

    for r0 in range(0, seq, rows):
        pos = pos0 + r0 + lax.broadcasted_iota(jnp.int32, (rows, 1), 0)
        for g, win in enumerate(POOL_WINDOWS):
            c = slice(g * POOL_GROUP_DIM, (g + 1) * POOL_GROUP_DIM)
            base = POOL_HIST_PAD + r0
            tok = extp_ref[base:base + rows, c]
            wsum = tok
            for k in range(1, win):
                wsum = wsum + extp_ref[base - k:base - k + rows, c]
            cnt = jnp.minimum(pos + 1, win).astype(F32)
            pooled = wsum / cnt - tok
            mixed = _dot(pooled.astype(BF16), pw_ref[g])
            yp_ref[r0:r0 + rows, c] = (mixed * ps_ref[:, c]).astype(BF16)
        base = CONV_HIST_PAD + r0
        y = cw_ref[0:1, :] * extc_ref[base - 2:base - 2 + rows, :]
        y = y + cw_ref[1:2, :] * extc_ref[base - 1:base - 1 + rows, :]
        y = y + cw_ref[2:3, :] * extc_ref[base:base + rows, :]
        yc_ref[r0:r0 + rows, :] = (cb_ref[r0:r0 + rows, :] * y).astype(BF16)


def _pool_conv(zp, cb, cc, cx, hist_pool, hist_conv, pool_w, pool_scale, conv_w, layer, *, pos0):
    b, seq, _ = zp.shape
    assert seq >= CONV_HIST and seq % 8 == 0
    rows = min(seq, 256)
    assert seq % rows == 0
    seq_block = pl.BlockSpec((None, seq, 512), lambda i: (i, 0, 0))
    return pl.pallas_call(
        functools.partial(_pool_conv_kernel, seq=seq, pos0=pos0, rows=rows),
        grid=(b,),
        in_specs=[
            seq_block, seq_block, seq_block, seq_block,
            pl.BlockSpec((None, POOL_HIST_PAD, POOL_WIDTH), lambda i: (i, 0, 0)),
            pl.BlockSpec((None, CONV_HIST_PAD, CONV_WIDTH), lambda i: (i, 0, 0)),
            pl.BlockSpec((None, len(POOL_WINDOWS), POOL_GROUP_DIM, POOL_GROUP_DIM),
                         lambda i: (layer, 0, 0, 0)),
            pl.BlockSpec((None, 1, POOL_WIDTH), lambda i: (layer, 0, 0)),
            pl.BlockSpec((None, CONV_K, CONV_WIDTH), lambda i: (layer, 0, 0)),
        ],
        out_specs=[
            pl.BlockSpec((None, seq, POOL_WIDTH), lambda i: (i, 0, 0)),
            pl.BlockSpec((None, seq, CONV_WIDTH), lambda i: (i, 0, 0)),
            pl.BlockSpec((None, CONV_HIST, CONV_WIDTH), lambda i: (i, 0, 0)),
        ],
        out_shape=[
            jax.ShapeDtypeStruct((b, seq, POOL_WIDTH), BF16),
            jax.ShapeDtypeStruct((b, seq, CONV_WIDTH), BF16),
            jax.ShapeDtypeStruct((b, CONV_HIST, CONV_WIDTH), F32),
        ],
        scratch_shapes=[pltpu.VMEM((POOL_HIST_PAD + seq, POOL_WIDTH), F32),
                        pltpu.VMEM((CONV_HIST_PAD + seq, CONV_WIDTH), F32)],
        compiler_params=_params(1),
        name="pool_conv",
    )(zp, cb, cc, cx, hist_pool, hist_conv, pool_w, pool_scale, conv_w)


def _diff_lambda(lq1_ref, lk1_ref, lq2_ref, lk2_ref, lam_init):
    a = jnp.sum(lq1_ref[...] * lk1_ref[...], axis=-1, keepdims=True)
    b = jnp.sum(lq2_ref[...] * lk2_ref[...], axis=-1, keepdims=True)
    return jnp.exp(a) - jnp.exp(b) + lam_init


def _diff_prompt_kernel(slopes_ref, q_ref, k_ref, v_ref, lq1_ref, lk1_ref, lq2_ref, lk2_ref, g_ref,
                        o_ref, bm_ref, mx_ref, l_ref, acc_ref, *, tq, tk, lam_init):
    h = pl.program_id(1)
    qi = pl.program_id(2)
    n_diag = tq // tk

    @pl.when((pl.program_id(0) == 0) & (h == 0) & (qi == 0))
    def _():
        ii = lax.broadcasted_iota(jnp.int32, (tq, tk), 0)
        jj = lax.broadcasted_iota(jnp.int32, (tq, tk), 1)
        for d in range(n_diag):
            kj = jj + d * tk
            base = (ii - jnp.abs(ii - kj)).astype(F32)
            bm_ref[d] = jnp.where(kj <= (ii | (CHUNK - 1)), base, -jnp.inf)

    qs = (q_ref[:, :DIFF_HEAD_DIM], q_ref[:, DIFF_HEAD_DIM:])
    scale = DIFF_HEAD_DIM ** -0.5
    slope = slopes_ref[h]
    col = lax.broadcasted_iota(jnp.int32, (1, tk), 1)

    def key_start(kt, d):
        return pl.multiple_of((kt if d is None else qi * n_diag + d) * tk, tk)

    def scores(kt, c, d):
        k0 = key_start(kt, d)
        kk = k_ref[pl.ds(k0, tk), c * DIFF_HEAD_DIM:(c + 1) * DIFF_HEAD_DIM]
        s = _dot_nt(qs[c], kk) * scale
        if d is None:
            return s + slope * (k0 - qi * tq + col).astype(F32)
        return s + slope * bm_ref[d]

    def earlier_tiles(fn):
        def body(i, carry):
            for u in range(n_diag):
                fn(i * n_diag + u, None)
            return carry
        lax.fori_loop(0, qi, body, 0)

    def pass1(kt, d):
        for c in range(2):
            s = scores(kt, c, d)
            mx_ref[c] = s if d == 0 else jnp.maximum(mx_ref[c], s)

    for d in range(n_diag):
        pass1(None, d)
    earlier_tiles(pass1)
    m_full = [jnp.broadcast_to(jnp.max(mx_ref[c], axis=-1, keepdims=True), (tq, tk)) for c in range(2)]

    def pass2(kt, d):
        vv = v_ref[pl.ds(key_start(kt, d), tk), :]
        for c in range(2):
            p = jnp.exp(scores(kt, c, d) - m_full[c])
            pv = _dot(p.astype(BF16), vv)
            if d == 0:
                l_ref[c] = p
                acc_ref[c] = pv
            else:
                l_ref[c] += p
                acc_ref[c] += pv

    for d in range(n_diag):
        pass2(None, d)
    earlier_tiles(pass2)

    lam = _diff_lambda(lq1_ref, lk1_ref, lq2_ref, lk2_ref, lam_init)
    l1 = jnp.sum(l_ref[0], axis=-1, keepdims=True)
    l2 = jnp.sum(l_ref[1], axis=-1, keepdims=True)
    o = acc_ref[0] / l1 - lam * (acc_ref[1] / l2)
    o_ref[...] = (_rms(o, g_ref[...]) * (1.0 - lam_init)).astype(BF16)


def _diff_lambda_specs(layer, n_axes):
    idx = {2: lambda a, b: (layer, 0, 0), 3: lambda a, b, c: (layer, 0, 0)}[n_axes]
    return [pl.BlockSpec((None, 1, DIFF_HEAD_DIM), idx)] * 4 + [pl.BlockSpec((None, 1, DIFF_V_DIM), idx)]


def _diff_attn_prompt(q3, kh, vh, slopes, lq1, lk1, lq2, lk2, subln_g, layer, *, tq, tk, lam_init):
    b, seq, _ = q3.shape
    tq = min(tq, seq)
    assert seq % tq == 0 and tq % tk == 0 and tk % CHUNK == 0 and CHUNK & (CHUNK - 1) == 0
    return pl.pallas_call(
        functools.partial(_diff_prompt_kernel, tq=tq, tk=tk, lam_init=lam_init),
        grid=(b, N_DIFF_HEADS, seq // tq),
        in_specs=[
            pl.BlockSpec(memory_space=pltpu.SMEM),
            pl.BlockSpec((None, tq, DIFF_V_DIM), lambda i, h, q: (i, q, h)),
            pl.BlockSpec((None, None, seq, DIFF_V_DIM), lambda i, h, q: (i, h, 0, 0)),
            pl.BlockSpec((None, None, seq, DIFF_V_DIM), lambda i, h, q: (i, h, 0, 0)),
        ] + _diff_lambda_specs(layer, 3),
        out_specs=pl.BlockSpec((None, tq, DIFF_V_DIM), lambda i, h, q: (i, q, h)),
        out_shape=jax.ShapeDtypeStruct((b, seq, DIFF_WIDTH), BF16),
        scratch_shapes=[
            pltpu.VMEM((tq // tk, tq, tk), F32),
            pltpu.VMEM((2, tq, tk), F32),
            pltpu.VMEM((2, tq, tk), F32),
            pltpu.VMEM((2, tq, DIFF_V_DIM), F32),
        ],
        compiler_params=_params(3),
        name="diff_attn_prompt",
    )(slopes, q3, kh, vh, lq1, lk1, lq2, lk2, subln_g)


def _diff_sample_kernel(slopes_ref, q_ref, kn_ref, vn_ref, kc_ref, vc_ref,
                        lq1_ref, lk1_ref, lq2_ref, lk2_ref, g_ref,
                        o_ref, m_ref, l_ref, acc_ref, *, past, seq, tk, lam_init):
    kt = pl.program_id(1)
    n_cache_tiles = past // tk

    @pl.when(kt == 0)
    def _():
        m_ref[...] = jnp.full(m_ref.shape, -jnp.inf, F32)
        l_ref[...] = jnp.zeros(l_ref.shape, F32)
        acc_ref[...] = jnp.zeros(acc_ref.shape, F32)

    scale = DIFF_HEAD_DIM ** -0.5

    def process(k_heads, v_heads, n_keys, new_keys):
        ii = lax.broadcasted_iota(jnp.int32, (seq, n_keys), 0)
        jj = lax.broadcasted_iota(jnp.int32, (seq, n_keys), 1)
        if new_keys:
            dist = jnp.abs(ii - jj)
        else:
            dist = ii - jj + (past - kt * tk)
        dist = dist.astype(F32)
        s_all = []
        for h in range(N_DIFF_HEADS):
            bias = -slopes_ref[h] * dist
            for c in range(2):
                lo = h * DIFF_V_DIM + c * DIFF_HEAD_DIM
                qc = q_ref[:, lo:lo + DIFF_HEAD_DIM]
                kc = k_heads[h][:, c * DIFF_HEAD_DIM:(c + 1) * DIFF_HEAD_DIM]
                s_all.append(_dot_nt(qc, kc) * scale + bias)
        s = jnp.stack(s_all)
        m_old = m_ref[...]
        m_new = jnp.maximum(m_old, jnp.max(s, axis=-1, keepdims=True))
        alpha = jnp.exp(m_old - m_new)
        p = jnp.exp(s - m_new)
        l_ref[...] = alpha * l_ref[...] + jnp.sum(p, axis=-1, keepdims=True)
        pb = p.astype(BF16)
        pv = jnp.stack([_dot(pb[i], v_heads[i // 2]) for i in range(2 * N_DIFF_HEADS)])
        acc_ref[...] = alpha * acc_ref[...] + pv
        m_ref[...] = m_new

    def heads_first(ref):
        x = jnp.swapaxes(ref[...], 0, 1).astype(BF16)
        return [x[h] for h in range(N_DIFF_HEADS)]

    @pl.when(kt < n_cache_tiles)
    def _():
        process(heads_first(kc_ref), heads_first(vc_ref), tk, False)

    @pl.when(kt == n_cache_tiles)
    def _():
        process(heads_first(kn_ref), heads_first(vn_ref), seq, True)
        lam = _diff_lambda(lq1_ref, lk1_ref, lq2_ref, lk2_ref, lam_init)
        for h in range(N_DIFF_HEADS):
            o = acc_ref[2 * h] / l_ref[2 * h] - lam * (acc_ref[2 * h + 1] / l_ref[2 * h + 1])
            y = _rms(o, g_ref[...]) * (1.0 - lam_init)
            o_ref[:, h * DIFF_V_DIM:(h + 1) * DIFF_V_DIM] = y.astype(BF16)


def _diff_attn_sample(q3, k_new, v_new, cache_k, cache_v, slopes, lq1, lk1, lq2, lk2, subln_g, layer, *, tk, lam_init):
    b, seq, _ = q3.shape
    past = cache_k.shape[2]
    assert past % tk == 0 and past % CHUNK == 0 and seq <= CHUNK
    n_cache_tiles = past // tk
    cache_spec = pl.BlockSpec((None, None, tk, N_DIFF_HEADS, DIFF_V_DIM),
                              lambda i, k: (layer, i, jnp.minimum(k, n_cache_tiles - 1), 0, 0))
    q_spec = pl.BlockSpec((None, seq, DIFF_WIDTH), lambda i, k: (i, 0, 0))
    new_spec = pl.BlockSpec((None, None, seq, N_DIFF_HEADS, DIFF_V_DIM), lambda i, k: (layer, i, 0, 0, 0))
    return pl.pallas_call(
        functools.partial(_diff_sample_kernel, past=past, seq=seq, tk=tk, lam_init=lam_init),
        grid=(b, n_cache_tiles + 1),
        in_specs=[
            pl.BlockSpec(memory_space=pltpu.SMEM),
            q_spec, new_spec, new_spec,
            cache_spec, cache_spec,
        ] + _diff_lambda_specs(layer, 2),
        out_specs=pl.BlockSpec((None, seq, DIFF_WIDTH), lambda i, k: (i, 0, 0)),
        out_shape=jax.ShapeDtypeStruct((b, seq, DIFF_WIDTH), BF16),
        scratch_shapes=[
            pltpu.VMEM((2 * N_DIFF_HEADS, seq, 1), F32),
            pltpu.VMEM((2 * N_DIFF_HEADS, seq, 1), F32),
            pltpu.VMEM((2 * N_DIFF_HEADS, seq, DIFF_V_DIM), F32),
        ],
        compiler_params=_params(2),
        name="diff_attn_sample",
    )(slopes, q3, k_new, v_new, cache_k, cache_v, lq1, lk1, lq2, lk2, subln_g)


def _out_proj_kernel(yp_ref, yd_ref, yc_ref, w_ref, x_ref, g_ref, o_ref):
    m = _dot(yp_ref[...], w_ref[0:POOL_WIDTH, :])
    m = m + _dot(yd_ref[...], w_ref[POOL_WIDTH:POOL_WIDTH + DIFF_WIDTH, :])
    m = m + _dot(yc_ref[...], w_ref[POOL_WIDTH + DIFF_WIDTH:, :])
    o_ref[...] = x_ref[...] + _rms(m, g_ref[...])


def _out_proj(yp, yd, yc, w_out, x, g_post, layer, *, tm):
    n, d = x.shape
    tm = min(tm, n)
    assert n % tm == 0
    row = lambda width: pl.BlockSpec((tm, width), lambda i: (i, 0))
    return pl.pallas_call(
        _out_proj_kernel,
        grid=(n // tm,),
        in_specs=[
            row(POOL_WIDTH), row(DIFF_WIDTH), row(CONV_WIDTH),
            pl.BlockSpec((None, w_out.shape[1], d), lambda i: (layer, 0, 0)),
            row(d),
            pl.BlockSpec((None, 1, d), lambda i: (layer, 0, 0)),
        ],
        out_specs=row(d),
        out_shape=jax.ShapeDtypeStruct((n, d), F32),
        compiler_params=_params(1),
        name="out_proj",
    )(yp, yd, yc, w_out, x, g_post)


def _mem_attn_kernel(x_ref, gpre_ref, wq_ref, mk_ref, mv_ref, wo_ref, gpost_ref, o_ref):
    x = x_ref[...]
    q = _dot(_rms(x, gpre_ref[...]).astype(BF16), wq_ref[...])
    scale = MEM_HEAD_DIM ** -0.5
    heads = []
    for h in range(N_MEM_HEADS):
        c = slice(h * MEM_HEAD_DIM, (h + 1) * MEM_HEAD_DIM)
        s = _dot_nt(q[:, c].astype(BF16), mk_ref[:, c].astype(BF16)) * scale
        p = jnp.exp(s - jnp.max(s, axis=-1, keepdims=True))
        p = p / jnp.sum(p, axis=-1, keepdims=True)
        heads.append(_dot(p.astype(BF16), mv_ref[:, c].astype(BF16)))
    o = jnp.concatenate(heads, axis=-1).astype(BF16)
    o_ref[...] = x + _rms(_dot(o, wo_ref[...]), gpost_ref[...])


def _mem_attn(x3, g_pre, w_mq, mk, mv, w_mo, g_post, layer, mem_layer, *, tm):
    b, seq, d = x3.shape
    mem_len = mk.shape[2]
    tm = min(tm, seq)
    assert seq % tm == 0
    mem_spec = pl.BlockSpec((None, None, mem_len, MEM_WIDTH), lambda i, r: (mem_layer, i, 0, 0))
    gain = pl.BlockSpec((None, 1, d), lambda i, r: (layer, 0, 0))
    return pl.pallas_call(
        _mem_attn_kernel,
        grid=(b, seq // tm),
        in_specs=[
            pl.BlockSpec((None, tm, d), lambda i, r: (i, r, 0)),
            gain,
            pl.BlockSpec((None, d, MEM_WIDTH), lambda i, r: (layer, 0, 0)),
            mem_spec, mem_spec,
            pl.BlockSpec((None, MEM_WIDTH, d), lambda i, r: (layer, 0, 0)),
            gain,
        ],
        out_specs=pl.BlockSpec((None, tm, d), lambda i, r: (i, r, 0)),
        out_shape=jax.ShapeDtypeStruct((b, seq, d), F32),
        compiler_params=_params(2),
        name="mem_attn",
    )(x3, g_pre, w_mq, mk, mv, w_mo, g_post)


def _pad_front(hist, rows):
    return jnp.pad(hist, ((0, 0), (rows - hist.shape[1], 0), (0, 0)))


def kernel(x_prompt, x_sample, mem_prompt, cache_diff_k, cache_diff_v, state_pool, state_conv, cache_mem_k, cache_mem_v, ffn1_pre_g, ffn1_post_g, ffn1_wg, ffn1_wu, ffn1_wd, mix_pre_g, mix_post_g, w_in, pool_w, pool_scale, diff_lq1, diff_lk1, diff_lq2, diff_lk2, diff_subln_g, conv_w, w_out, mem_pre_g, mem_post_g, mem_kv_g, w_mq, w_mk, w_mv, w_mo, ffn2_pre_g, ffn2_post_g, ffn2_wg, ffn2_wu, ffn2_wd):
    n_layers = w_in.shape[0]
    bp, sp, d = x_prompt.shape
    bs, ss, _ = x_sample.shape
    past = cache_diff_k.shape[2]
    mem_len = mem_prompt.shape[1]
    assert sp >= POOL_HIST and ss >= POOL_HIST

    bf = lambda w: w.astype(BF16)
    row = lambda v: v.reshape(n_layers, 1, v.shape[-1])
    ffn1 = (row(ffn1_pre_g), bf(ffn1_wg), bf(ffn1_wu), bf(ffn1_wd), row(ffn1_post_g))
    ffn2 = (row(ffn2_pre_g), bf(ffn2_wg), bf(ffn2_wu), bf(ffn2_wd), row(ffn2_post_g))
    mix_pre_g, mix_post_g = row(mix_pre_g), row(mix_post_g)
    mem_pre_g, mem_post_g, mem_kv_g = row(mem_pre_g), row(mem_post_g), row(mem_kv_g)
    w_in, w_out, w_mq, w_mk, w_mv, w_mo, pool_w = map(bf, (_permute_in_proj(w_in), w_out, w_mq, w_mk, w_mv, w_mo, pool_w))
    pool_scale = row(pool_scale)
    lam_vecs = tuple(row(v) for v in (diff_lq1, diff_lk1, diff_lq2, diff_lk2, diff_subln_g))
    slopes = jnp.asarray([2.0 ** (-8.0 * (h + 1) / N_DIFF_HEADS) for h in range(N_DIFF_HEADS)], F32)

    mk_p, mv_p = _mem_kv(mem_prompt.reshape(bp * mem_len, d), mem_kv_g, w_mk, w_mv, tm=1024)
    mk_p = mk_p.reshape(n_layers, bp, mem_len, MEM_WIDTH)
    mv_p = mv_p.reshape(n_layers, bp, mem_len, MEM_WIDTH)
    mk_s = cache_mem_k.reshape(n_layers, bs, mem_len, MEM_WIDTH)
    mv_s = cache_mem_v.reshape(n_layers, bs, mem_len, MEM_WIDTH)

    zero_pool = jnp.zeros((bp, POOL_HIST_PAD, POOL_WIDTH), F32)
    zero_conv = jnp.zeros((bp, CONV_HIST_PAD, CONV_WIDTH), F32)

    xp = x_prompt.reshape(bp * sp, d)
    xs = x_sample.reshape(bs * ss, d)
    small = {name: [] for name in ("pp", "cp", "ps", "cs")}
    stacks = {"p": (None, None), "s": (None, None)}

    def layer_step(x, b, seq, l, group, hist_pool, hist_conv, pos0, mk, mv, mem_tm):
        lam_init = 0.8 - 0.6 * math.exp(-0.3 * l)
        prompt = group == "p"
        x = _ffn(x, *ffn1, l, tm=512, tf=512)
        proj = _in_proj(x, mix_pre_g, w_in, l, *stacks[group], n_layers=n_layers, seq=seq, tm=512, head_major=prompt)
        q, k_stack, v_stack = proj[:3]
        stacks[group] = (k_stack, v_stack)
        zp, cb, cc, cx = (a.reshape(b, seq, -1) for a in proj[-4:])
        y_pool, y_conv, new_conv = _pool_conv(zp, cb, cc, cx, hist_pool, hist_conv, pool_w, pool_scale, conv_w, l,
                                              pos0=pos0)
        q3 = q.reshape(b, seq, DIFF_WIDTH)
        if prompt:
            y_diff = _diff_attn_prompt(q3, proj[3], proj[4], slopes, *lam_vecs, l, tq=512, tk=256, lam_init=lam_init)
        else:
            rows5 = (n_layers, b, seq, N_DIFF_HEADS, DIFF_V_DIM)
            y_diff = _diff_attn_sample(q3, k_stack.reshape(rows5), v_stack.reshape(rows5), cache_diff_k, cache_diff_v,
                                       slopes, *lam_vecs, l, tk=min(1024, past), lam_init=lam_init)
        x = _out_proj(y_pool.reshape(b * seq, -1), y_diff.reshape(b * seq, -1), y_conv.reshape(b * seq, -1),
                      w_out, x, mix_post_g, l, tm=512)
        x = _mem_attn(x.reshape(b, seq, d), mem_pre_g, w_mq, mk, mv, w_mo, mem_post_g, l, l, tm=mem_tm)
        x = _ffn(x.reshape(b * seq, d), *ffn2, l, tm=512, tf=512)
        return x, zp[:, seq - POOL_HIST:, :], new_conv

    for l in range(n_layers):
        xp, pp, cp = layer_step(xp, bp, sp, l, "p", zero_pool, zero_conv, 0, mk_p, mv_p, 512)
        xs, ps, cs = layer_step(xs, bs, ss, l, "s", _pad_front(state_pool[l], POOL_HIST_PAD),
                                _pad_front(state_conv[l], CONV_HIST_PAD), past, mk_s, mv_s, ss)
        for name, val in zip(("pp", "cp", "ps", "cs"), (pp, cp, ps, cs)):
            small[name].append(val)

    st = lambda name: jnp.stack(small[name])
    rows_p = (n_layers, bp, sp, N_DIFF_HEADS, DIFF_V_DIM)
    rows_s = (n_layers, bs, ss, N_DIFF_HEADS, DIFF_V_DIM)
    head_shape = (n_layers, bp, mem_len, N_MEM_HEADS, MEM_HEAD_DIM)
    return (xp.reshape(bp, sp, d), xs.reshape(bs, ss, d),
            stacks["p"][0].reshape(rows_p), stacks["p"][1].reshape(rows_p), st("pp"), st("cp"),
            mk_p.reshape(head_shape), mv_p.reshape(head_shape),
            stacks["s"][0].reshape(rows_s), stacks["s"][1].reshape(rows_s), st("ps"), st("cs"))
```

```python
import functools
import math

import jax
import jax.numpy as jnp
from jax import lax
from jax.experimental import pallas as pl
from jax.experimental.pallas import tpu as pltpu

F32 = jnp.float32
BF16 = jnp.bfloat16

EPS = 1e-6
CHUNK = 64
POOL_WINDOWS = (2, 4, 8, 16)
POOL_GROUP_DIM = 128
POOL_WIDTH = POOL_GROUP_DIM * len(POOL_WINDOWS)
POOL_HIST = max(POOL_WINDOWS) - 1
POOL_HIST_PAD = 16
N_DIFF_HEADS = 4
DIFF_HEAD_DIM = 128
DIFF_V_DIM = 2 * DIFF_HEAD_DIM
DIFF_WIDTH = N_DIFF_HEADS * DIFF_V_DIM
CONV_WIDTH = 512
CONV_K = 3
CONV_HIST = CONV_K - 1
CONV_HIST_PAD = 8
N_MEM_HEADS = 4
MEM_HEAD_DIM = 128
MEM_WIDTH = N_MEM_HEADS * MEM_HEAD_DIM
MEM_ROWS_PER_STEP = 512
OFF_Q = POOL_WIDTH
OFF_K = OFF_Q + DIFF_WIDTH
OFF_V = OFF_K + DIFF_WIDTH
OFF_CB = OFF_V + DIFF_WIDTH
OFF_CC = OFF_CB + CONV_WIDTH
OFF_CX = OFF_CC + CONV_WIDTH

V7X_VMEM_LIMIT_BYTES = 56 * 1024 * 1024


def _params(n_axes):
    return pltpu.CompilerParams(dimension_semantics=("arbitrary",) * n_axes,
                                vmem_limit_bytes=V7X_VMEM_LIMIT_BYTES)


def _rms(xf, g):
    ms = jnp.mean(xf * xf, axis=-1, keepdims=True)
    return xf * lax.rsqrt(ms + EPS) * g


def _dot(a, b):
    return jnp.dot(a, b, preferred_element_type=F32)


def _dot_nt(a, b):
    return lax.dot_general(a, b, (((1,), (1,)), ((), ())), preferred_element_type=F32)


def _ffn_kernel(x_ref, gpre_ref, wg_ref, wu_ref, wd_ref, gpost_ref, y_ref, xn_ref, *, n_ff_steps):
    j = pl.program_id(1)

    @pl.when(j == 0)
    def _():
        xn_ref[...] = _rms(x_ref[...], gpre_ref[...]).astype(BF16)
        y_ref[...] = jnp.zeros(y_ref.shape, F32)

    xn = xn_ref[...]
    gate = _dot(xn, wg_ref[...])
    up = _dot(xn, wu_ref[...])
    h = (gate * jax.nn.sigmoid(gate)) * up
    y_ref[...] += _dot(h.astype(BF16), wd_ref[...])

    @pl.when(j == n_ff_steps - 1)
    def _():
        y_ref[...] = x_ref[...] + 0.5 * _rms(y_ref[...], gpost_ref[...])


def _ffn(x, g_pre, wg, wu, wd, g_post, layer, *, tm, tf):
    n, d = x.shape
    f = wg.shape[-1]
    tm, tf = min(tm, n), min(tf, f)
    assert n % tm == 0 and f % tf == 0
    return pl.pallas_call(
        functools.partial(_ffn_kernel, n_ff_steps=f // tf),
        grid=(n // tm, f // tf),
        in_specs=[
            pl.BlockSpec((tm, d), lambda i, j: (i, 0)),
            pl.BlockSpec((None, 1, d), lambda i, j: (layer, 0, 0)),
            pl.BlockSpec((None, d, tf), lambda i, j: (layer, 0, j)),
            pl.BlockSpec((None, d, tf), lambda i, j: (layer, 0, j)),
            pl.BlockSpec((None, tf, d), lambda i, j: (layer, j, 0)),
            pl.BlockSpec((None, 1, d), lambda i, j: (layer, 0, 0)),
        ],
        out_specs=pl.BlockSpec((tm, d), lambda i, j: (i, 0)),
        out_shape=jax.ShapeDtypeStruct((n, d), F32),
        scratch_shapes=[pltpu.VMEM((tm, d), BF16)],
        compiler_params=_params(2),
        name="ffn",
    )(x, g_pre, wg, wu, wd, g_post)


IN_PROJ_BLOCK = 1024
IN_PROJ_ORDER = ((OFF_Q, OFF_K), (OFF_K, OFF_V), (OFF_V, OFF_CB), (0, OFF_Q), (OFF_CB, OFF_CC), (OFF_CC, OFF_CX),
                 (OFF_CX, OFF_CX + CONV_WIDTH))


def _permute_in_proj(w_in):
    return jnp.concatenate([w_in[..., lo:hi] for lo, hi in IN_PROJ_ORDER], axis=-1)


def _store_heads(z_ref, rows_ref, heads_ref):
    heads = jnp.stack([z_ref[:, h * DIFF_V_DIM:(h + 1) * DIFF_V_DIM] for h in range(N_DIFF_HEADS)])
    rows_ref[...] = jnp.swapaxes(heads, 0, 1)
    if heads_ref is not None:
        heads_ref[...] = heads.astype(BF16)


def _in_proj_kernel(*refs, head_major, aliased):
    refs = list(refs)
    x_ref, g_ref, w_ref = refs[:3]
    refs = refs[3 + (2 if aliased else 0):]
    q_ref, kf_ref, vf_ref = refs[:3]
    refs = refs[3:]
    kb_ref, vb_ref = (refs[0], refs[1]) if head_major else (None, None)
    refs = refs[2 if head_major else 0:]
    zp_ref, cb_ref, cc_ref, cx_ref, xn_ref, zk_ref, zv_ref = refs
    j = pl.program_id(1)
    half = IN_PROJ_BLOCK // 2

    @pl.when(j == 0)
    def _():
        xn_ref[...] = _rms(x_ref[...], g_ref[...]).astype(BF16)
        q_ref[...] = _dot(xn_ref[...], w_ref[...]).astype(BF16)

    @pl.when(j == 1)
    def _():
        zk_ref[...] = _dot(xn_ref[...], w_ref[...])

    @pl.when(j == 2)
    def _():
        zv_ref[...] = _dot(xn_ref[...], w_ref[...])
        _store_heads(zk_ref, kf_ref, kb_ref)

    @pl.when(j == 3)
    def _():
        z = _dot(xn_ref[...], w_ref[...])
        zp_ref[...] = z[:, :half]
        cb_ref[...] = z[:, half:]
        _store_heads(zv_ref, vf_ref, vb_ref)

    @pl.when(j == 4)
    def _():
        z = _dot(xn_ref[...], w_ref[...])
        cc_ref[...] = z[:, :half]
        cx_ref[...] = z[:, half:]


def _in_proj(x, g, w_perm, layer, k_stack, v_stack, *, n_layers, seq, tm, head_major):
    n, d = x.shape
    tm = min(tm, n, seq) if head_major else min(tm, n)
    assert n % tm == 0 and w_perm.shape[-1] == 5 * IN_PROJ_BLOCK and (not head_major or seq % tm == 0)
    aliased = k_stack is not None
    rows = lambda width, dt: (pl.BlockSpec((tm, width), lambda i, j: (i, 0)), jax.ShapeDtypeStruct((n, width), dt))
    stack = (pl.BlockSpec((None, tm, N_DIFF_HEADS, DIFF_V_DIM), lambda i, j: (layer, i, 0, 0)),
             jax.ShapeDtypeStruct((n_layers, n, N_DIFF_HEADS, DIFF_V_DIM), F32))
    tiles = seq // tm if head_major else 1
    heads = (pl.BlockSpec((None, N_DIFF_HEADS, tm, DIFF_V_DIM), lambda i, j: (i // tiles, 0, i % tiles, 0)),
             jax.ShapeDtypeStruct((n // seq, N_DIFF_HEADS, seq, DIFF_V_DIM), BF16))
    outs = [rows(DIFF_WIDTH, BF16), stack, stack] + ([heads, heads] if head_major else []) + [rows(512, F32)] * 4
    in_specs = [
        pl.BlockSpec((tm, d), lambda i, j: (i, 0)),
        pl.BlockSpec((None, 1, d), lambda i, j: (layer, 0, 0)),
        pl.BlockSpec((None, d, IN_PROJ_BLOCK), lambda i, j: (layer, 0, j)),
    ]
    args = [x, g, w_perm]
    if aliased:
        in_specs += [pl.BlockSpec(memory_space=pl.ANY)] * 2
        args += [k_stack, v_stack]
    return pl.pallas_call(
        functools.partial(_in_proj_kernel, head_major=head_major, aliased=aliased),
        grid=(n // tm, 5),
        in_specs=in_specs,
        out_specs=[o[0] for o in outs],
        out_shape=[o[1] for o in outs],
        input_output_aliases={3: 1, 4: 2} if aliased else {},
        scratch_shapes=[pltpu.VMEM((tm, d), BF16), pltpu.VMEM((tm, DIFF_WIDTH), F32), pltpu.VMEM((tm, DIFF_WIDTH), F32)],
        compiler_params=_params(2),
        name="in_proj",
    )(*args)


def _mem_kv_kernel(x_ref, g_ref, wk_ref, wv_ref, k_ref, v_ref):
    xn = _rms(x_ref[...], g_ref[...]).astype(BF16)
    k_ref[...] = _dot(xn, wk_ref[...])
    v_ref[...] = _dot(xn, wv_ref[...])


def _mem_kv(mem, g, wk, wv, *, tm):
    r, d = mem.shape
    n_layers = wk.shape[0]
    tm = min(tm, r)
    assert r % tm == 0
    out = jax.ShapeDtypeStruct((n_layers, r, MEM_WIDTH), F32)
    return pl.pallas_call(
        _mem_kv_kernel,
        grid=(n_layers, r // tm),
        in_specs=[
            pl.BlockSpec((tm, d), lambda l, i: (i, 0)),
            pl.BlockSpec((None, 1, d), lambda l, i: (l, 0, 0)),
            pl.BlockSpec((None, d, MEM_WIDTH), lambda l, i: (l, 0, 0)),
            pl.BlockSpec((None, d, MEM_WIDTH), lambda l, i: (l, 0, 0)),
        ],
        out_specs=[pl.BlockSpec((None, tm, MEM_WIDTH), lambda l, i: (l, i, 0))] * 2,
        out_shape=[out, out],
        compiler_params=_params(2),
        name="mem_kv",
    )(mem, g, wk, wv)


def _pool_conv_kernel(zp_ref, cb_ref, cc_ref, cx_ref, hp_ref, hc_ref, pw_ref, ps_ref, cw_ref,
                      yp_ref, yc_ref, nc_ref, extp_ref, extc_ref, *, seq, pos0, rows):
    extp_ref[0:POOL_HIST_PAD, :] = hp_ref[...]
    extp_ref[POOL_HIST_PAD:POOL_HIST_PAD + seq, :] = zp_ref[...]
    extc_ref[0:CONV_HIST_PAD, :] = hc_ref[...]
    extc_ref[CONV_HIST_PAD:CONV_HIST_PAD + seq, :] = cc_ref[...] * cx_ref[...]
    nc_ref[...] = extc_ref[CONV_HIST_PAD + seq - CONV_HIST:CONV_HIST_PAD + seq, :]

    for r0 in range(0, seq, rows):
        pos = pos0 + r0 + lax.broadcasted_iota(jnp.int32, (rows, 1), 0)
        for g, win in enumerate(POOL_WINDOWS):
            c = slice(g * POOL_GROUP_DIM, (g + 1) * POOL_GROUP_DIM)
            base = POOL_HIST_PAD + r0
            tok = extp_ref[base:base + rows, c]
            wsum = tok
            for k in range(1, win):
                wsum = wsum + extp_ref[base - k:base - k + rows, c]
            cnt = jnp.minimum(pos + 1, win).astype(F32)
            pooled = wsum / cnt - tok
            mixed = _dot(pooled.astype(BF16), pw_ref[g])
            yp_ref[r0:r0 + rows, c] = (mixed * ps_ref[:, c]).astype(BF16)
        base = CONV_HIST_PAD + r0
        y = cw_ref[0:1, :] * extc_ref[base - 2:base - 2 + rows, :]
        y = y + cw_ref[1:2, :] * extc_ref[base - 1:base - 1 + rows, :]
        y = y + cw_ref[2:3, :] * extc_ref[base:base + rows, :]
        yc_ref[r0:r0 + rows, :] = (cb_ref[r0:r0 + rows, :] * y).astype(BF16)


def _pool_conv(zp, cb, cc, cx, hist_pool, hist_conv, pool_w, pool_scale, conv_w, layer, *, pos0):
    b, seq, _ = zp.shape
    assert seq >= CONV_HIST and seq % 8 == 0
    rows = min(seq, 256)
    assert seq % rows == 0
    seq_block = pl.BlockSpec((None, seq, 512), lambda i: (i, 0, 0))
    return pl.pallas_call(
        functools.partial(_pool_conv_kernel, seq=seq, pos0=pos0, rows=rows),
        grid=(b,),
        in_specs=[
            seq_block, seq_block, seq_block, seq_block,
            pl.BlockSpec((None, POOL_HIST_PAD, POOL_WIDTH), lambda i: (i, 0, 0)),
            pl.BlockSpec((None, CONV_HIST_PAD, CONV_WIDTH), lambda i: (i, 0, 0)),
            pl.BlockSpec((None, len(POOL_WINDOWS), POOL_GROUP_DIM, POOL_GROUP_DIM),
                         lambda i: (layer, 0, 0, 0)),
            pl.BlockSpec((None, 1, POOL_WIDTH), lambda i: (layer, 0, 0)),
            pl.BlockSpec((None, CONV_K, CONV_WIDTH), lambda i: (layer, 0, 0)),
        ],
        out_specs=[
            pl.BlockSpec((None, seq, POOL_WIDTH), lambda i: (i, 0, 0)),
            pl.BlockSpec((None, seq, CONV_WIDTH), lambda i: (i, 0, 0)),
            pl.BlockSpec((None, CONV_HIST, CONV_WIDTH), lambda i: (i, 0, 0)),
        ],
        out_shape=[
            jax.ShapeDtypeStruct((b, seq, POOL_WIDTH), BF16),
            jax.ShapeDtypeStruct((b, seq, CONV_WIDTH), BF16),
            jax.ShapeDtypeStruct((b, CONV_HIST, CONV_WIDTH), F32),
        ],
        scratch_shapes=[pltpu.VMEM((POOL_HIST_PAD + seq, POOL_WIDTH), F32),
                        pltpu.VMEM((CONV_HIST_PAD + seq, CONV_WIDTH), F32)],
        compiler_params=_params(1),
        name="pool_conv",
    )(zp, cb, cc, cx, hist_pool, hist_conv, pool_w, pool_scale, conv_w)


def _diff_lambda(lq1_ref, lk1_ref, lq2_ref, lk2_ref, lam_init):
    a = jnp.sum(lq1_ref[...] * lk1_ref[...], axis=-1, keepdims=True)
    b = jnp.sum(lq2_ref[...] * lk2_ref[...], axis=-1, keepdims=True)
    return jnp.exp(a) - jnp.exp(b) + lam_init


def _diff_prompt_kernel(slopes_ref, q_ref, k_ref, v_ref, lq1_ref, lk1_ref, lq2_ref, lk2_ref, g_ref,
                        o_ref, bm_ref, s_ref, mx_ref, l_ref, acc_ref, *, tq, tk, lam_init):
    h = pl.program_id(1)
    qi = pl.program_id(2)
    n_diag = tq // tk

    @pl.when((pl.program_id(0) == 0) & (h == 0) & (qi == 0))
    def _():
        ii = lax.broadcasted_iota(jnp.int32, (tq, tk), 0)
        jj = lax.broadcasted_iota(jnp.int32, (tq, tk), 1)
        for d in range(n_diag):
            kj = jj + d * tk
            base = (ii - jnp.abs(ii - kj)).astype(F32)
            bm_ref[d] = jnp.where(kj <= (ii | (CHUNK - 1)), base, -jnp.inf)

    qs = (q_ref[:, :DIFF_HEAD_DIM], q_ref[:, DIFF_HEAD_DIM:])
    scale = DIFF_HEAD_DIM ** -0.5
    slope = slopes_ref[h]
    col = lax.broadcasted_iota(jnp.int32, (1, tk), 1)

    def tile_index(kt, d):
        return kt if d is None else qi * n_diag + d

    def key_start(kt, d):
        return pl.multiple_of(tile_index(kt, d) * tk, tk)

    def scores(kt, c, d):
        k0 = key_start(kt, d)
        kk = k_ref[pl.ds(k0, tk), c * DIFF_HEAD_DIM:(c + 1) * DIFF_HEAD_DIM]
        s = _dot_nt(qs[c], kk) * scale
        if d is None:
            return s + slope * (k0 - qi * tq + col).astype(F32)
        return s + slope * bm_ref[d]

    def earlier_tiles(fn):
        def body(i, carry):
            for u in range(n_diag):
                fn(i * n_diag + u, None)
            return carry
        lax.fori_loop(0, qi, body, 0)

    def pass1(kt, d):
        for c in range(2):
            s = scores(kt, c, d)
            s_ref[c, tile_index(kt, d)] = s
            mx_ref[c] = s if d == 0 else jnp.maximum(mx_ref[c], s)

    for d in range(n_diag):
        pass1(None, d)
    earlier_tiles(pass1)
    m_full = [jnp.broadcast_to(jnp.max(mx_ref[c], axis=-1, keepdims=True), (tq, tk)) for c in range(2)]

    def pass2(kt, d):
        vv = v_ref[pl.ds(key_start(kt, d), tk), :]
        for c in range(2):
            p = jnp.exp(s_ref[c, tile_index(kt, d)] - m_full[c])
            pv = _dot(p.astype(BF16), vv)
            if d == 0:
                l_ref[c] = p
                acc_ref[c] = pv
            else:
                l_ref[c] += p
                acc_ref[c] += pv

    for d in range(n_diag):
        pass2(None, d)
    earlier_tiles(pass2)

    lam = _diff_lambda(lq1_ref, lk1_ref, lq2_ref, lk2_ref, lam_init)
    l1 = jnp.sum(l_ref[0], axis=-1, keepdims=True)
    l2 = jnp.sum(l_ref[1], axis=-1, keepdims=True)
    o = acc_ref[0] / l1 - lam * (acc_ref[1] / l2)
    o_ref[...] = (_rms(o, g_ref[...]) * (1.0 - lam_init)).astype(BF16)


def _diff_lambda_specs(layer, n_axes):
    idx = {2: lambda a, b: (layer, 0, 0), 3: lambda a, b, c: (layer, 0, 0)}[n_axes]
    return [pl.BlockSpec((None, 1, DIFF_HEAD_DIM), idx)] * 4 + [pl.BlockSpec((None, 1, DIFF_V_DIM), idx)]


def _diff_attn_prompt(q3, kh, vh, slopes, lq1, lk1, lq2, lk2, subln_g, layer, *, tq, tk, lam_init):
    b, seq, _ = q3.shape
    tq = min(tq, seq)
    assert seq % tq == 0 and tq % tk == 0 and tk % CHUNK == 0 and CHUNK & (CHUNK - 1) == 0
    return pl.pallas_call(
        functools.partial(_diff_prompt_kernel, tq=tq, tk=tk, lam_init=lam_init),
        grid=(b, N_DIFF_HEADS, seq // tq),
        in_specs=[
            pl.BlockSpec(memory_space=pltpu.SMEM),
            pl.BlockSpec((None, tq, DIFF_V_DIM), lambda i, h, q: (i, q, h)),
            pl.BlockSpec((None, None, seq, DIFF_V_DIM), lambda i, h, q: (i, h, 0, 0)),
            pl.BlockSpec((None, None, seq, DIFF_V_DIM), lambda i, h, q: (i, h, 0, 0)),
        ] + _diff_lambda_specs(layer, 3),
        out_specs=pl.BlockSpec((None, tq, DIFF_V_DIM), lambda i, h, q: (i, q, h)),
        out_shape=jax.ShapeDtypeStruct((b, seq, DIFF_WIDTH), BF16),
        scratch_shapes=[
            pltpu.VMEM((tq // tk, tq, tk), F32),
            pltpu.VMEM((2, seq // tk, tq, tk), F32),
            pltpu.VMEM((2, tq, tk), F32),
            pltpu.VMEM((2, tq, tk), F32),
            pltpu.VMEM((2, tq, DIFF_V_DIM), F32),
        ],
        compiler_params=_params(3),
        name="diff_attn_prompt",
    )(slopes, q3, kh, vh, lq1, lk1, lq2, lk2, subln_g)


def _diff_sample_kernel(slopes_ref, q_ref, kn_ref, vn_ref, kc_ref, vc_ref,
                        lq1_ref, lk1_ref, lq2_ref, lk2_ref, g_ref,
                        o_ref, m_ref, l_ref, acc_ref, *, past, seq, tk, lam_init):
    kt = pl.program_id(1)
    n_cache_tiles = past // tk

    @pl.when(kt == 0)
    def _():
        m_ref[...] = jnp.full(m_ref.shape, -jnp.inf, F32)
        l_ref[...] = jnp.zeros(l_ref.shape, F32)
        acc_ref[...] = jnp.zeros(acc_ref.shape, F32)

    scale = DIFF_HEAD_DIM ** -0.5

    def process(k_heads, v_heads, n_keys, new_keys):
        ii = lax.broadcasted_iota(jnp.int32, (seq, n_keys), 0)
        jj = lax.broadcasted_iota(jnp.int32, (seq, n_keys), 1)
        if new_keys:
            dist = jnp.abs(ii - jj)
        else:
            dist = ii - jj + (past - kt * tk)
        dist = dist.astype(F32)
        s_all = []
        for h in range(N_DIFF_HEADS):
            bias = -slopes_ref[h] * dist
            for c in range(2):
                lo = h * DIFF_V_DIM + c * DIFF_HEAD_DIM
                qc = q_ref[:, lo:lo + DIFF_HEAD_DIM]
                kc = k_heads[h][:, c * DIFF_HEAD_DIM:(c + 1) * DIFF_HEAD_DIM]
                s_all.append(_dot_nt(qc, kc) * scale + bias)
        s = jnp.stack(s_all)
        m_old = m_ref[...]
        m_new = jnp.maximum(m_old, jnp.max(s, axis=-1, keepdims=True))
        alpha = jnp.exp(m_old - m_new)
        p = jnp.exp(s - m_new)
        l_ref[...] = alpha * l_ref[...] + jnp.sum(p, axis=-1, keepdims=True)
        pb = p.astype(BF16)
        pv = jnp.stack([_dot(pb[i], v_heads[i // 2]) for i in range(2 * N_DIFF_HEADS)])
        acc_ref[...] = alpha * acc_ref[...] + pv
        m_ref[...] = m_new

    def heads_first(ref):
        x = jnp.swapaxes(ref[...], 0, 1).astype(BF16)
        return [x[h] for h in range(N_DIFF_HEADS)]

    @pl.when(kt < n_cache_tiles)
    def _():
        process(heads_first(kc_ref), heads_first(vc_ref), tk, False)

    @pl.when(kt == n_cache_tiles)
    def _():
        process(heads_first(kn_ref), heads_first(vn_ref), seq, True)
        lam = _diff_lambda(lq1_ref, lk1_ref, lq2_ref, lk2_ref, lam_init)
        for h in range(N_DIFF_HEADS):
            o = acc_ref[2 * h] / l_ref[2 * h] - lam * (acc_ref[2 * h + 1] / l_ref[2 * h + 1])
            y = _rms(o, g_ref[...]) * (1.0 - lam_init)
            o_ref[:, h * DIFF_V_DIM:(h + 1) * DIFF_V_DIM] = y.astype(BF16)


def _diff_attn_sample(q3, k_new, v_new, cache_k, cache_v, slopes, lq1, lk1, lq2, lk2, subln_g, layer, *, tk, lam_init):
    b, seq, _ = q3.shape
    past = cache_k.shape[2]
    assert past % tk == 0 and past % CHUNK == 0 and seq <= CHUNK
    n_cache_tiles = past // tk
    cache_spec = pl.BlockSpec((None, None, tk, N_DIFF_HEADS, DIFF_V_DIM),
                              lambda i, k: (layer, i, jnp.minimum(k, n_cache_tiles - 1), 0, 0))
    q_spec = pl.BlockSpec((None, seq, DIFF_WIDTH), lambda i, k: (i, 0, 0))
    new_spec = pl.BlockSpec((None, None, seq, N_DIFF_HEADS, DIFF_V_DIM), lambda i, k: (layer, i, 0, 0, 0))
    return pl.pallas_call(
        functools.partial(_diff_sample_kernel, past=past, seq=seq, tk=tk, lam_init=lam_init),
        grid=(b, n_cache_tiles + 1),
        in_specs=[
            pl.BlockSpec(memory_space=pltpu.SMEM),
            q_spec, new_spec, new_spec,
            cache_spec, cache_spec,
        ] + _diff_lambda_specs(layer, 2),
        out_specs=pl.BlockSpec((None, seq, DIFF_WIDTH), lambda i, k: (i, 0, 0)),
        out_shape=jax.ShapeDtypeStruct((b, seq, DIFF_WIDTH), BF16),
        scratch_shapes=[
            pltpu.VMEM((2 * N_DIFF_HEADS, seq, 1), F32),
            pltpu.VMEM((2 * N_DIFF_HEADS, seq, 1), F32),
            pltpu.VMEM((2 * N_DIFF_HEADS, seq, DIFF_V_DIM), F32),
        ],
        compiler_params=_params(2),
        name="diff_attn_sample",
    )(slopes, q3, k_new, v_new, cache_k, cache_v, lq1, lk1, lq2, lk2, subln_g)


def _out_proj_kernel(yp_ref, yd_ref, yc_ref, w_ref, x_ref, g_ref, o_ref):
    m = _dot(yp_ref[...], w_ref[0:POOL_WIDTH, :])
    m = m + _dot(yd_ref[...], w_ref[POOL_WIDTH:POOL_WIDTH + DIFF_WIDTH, :])
    m = m + _dot(yc_ref[...], w_ref[POOL_WIDTH + DIFF_WIDTH:, :])
    o_ref[...] = x_ref[...] + _rms(m, g_ref[...])


def _out_proj(yp, yd, yc, w_out, x, g_post, layer, *, tm):
    n, d = x.shape
    tm = min(tm, n)
    assert n % tm == 0
    row = lambda width: pl.BlockSpec((tm, width), lambda i: (i, 0))
    return pl.pallas_call(
        _out_proj_kernel,
        grid=(n // tm,),
        in_specs=[
            row(POOL_WIDTH), row(DIFF_WIDTH), row(CONV_WIDTH),
            pl.BlockSpec((None, w_out.shape[1], d), lambda i: (layer, 0, 0)),
            row(d),
            pl.BlockSpec((None, 1, d), lambda i: (layer, 0, 0)),
        ],
        out_specs=row(d),
        out_shape=jax.ShapeDtypeStruct((n, d), F32),
        compiler_params=_params(1),
        name="out_proj",
    )(yp, yd, yc, w_out, x, g_post)


def _mem_attn_kernel(x_ref, gpre_ref, wq_ref, mk_ref, mv_ref, wo_ref, gpost_ref, o_ref):
    nb, tm, d = x_ref.shape
    x = x_ref[...].reshape(nb * tm, d)
    q = _dot(_rms(x, gpre_ref[...]).astype(BF16), wq_ref[...]).astype(BF16)
    scale = MEM_HEAD_DIM ** -0.5
    rows = []
    for i in range(nb):
        heads = []
        for h in range(N_MEM_HEADS):
            c = slice(h * MEM_HEAD_DIM, (h + 1) * MEM_HEAD_DIM)
            s = _dot_nt(q[i * tm:(i + 1) * tm, c], mk_ref[i, :, c].astype(BF16)) * scale
            p = jnp.exp(s - jnp.max(s, axis=-1, keepdims=True))
            p = p / jnp.sum(p, axis=-1, keepdims=True)
            heads.append(_dot(p.astype(BF16), mv_ref[i, :, c].astype(BF16)))
        rows.append(jnp.concatenate(heads, axis=-1))
    o = jnp.concatenate(rows, axis=0).astype(BF16)
    y = x + _rms(_dot(o, wo_ref[...]), gpost_ref[...])
    o_ref[...] = y.reshape(nb, tm, d)


def _mem_attn(x3, g_pre, w_mq, mk, mv, w_mo, g_post, layer, mem_layer, *, tm):
    b, seq, d = x3.shape
    mem_len = mk.shape[2]
    tm = min(tm, seq)
    nb = math.gcd(b, max(1, MEM_ROWS_PER_STEP // tm))
    assert seq % tm == 0 and tm % 8 == 0
    mem_spec = pl.BlockSpec((None, nb, mem_len, MEM_WIDTH), lambda i, r: (mem_layer, i, 0, 0))
    gain = pl.BlockSpec((None, 1, d), lambda i, r: (layer, 0, 0))
    return pl.pallas_call(
        _mem_attn_kernel,
        grid=(b // nb, seq // tm),
        in_specs=[
            pl.BlockSpec((nb, tm, d), lambda i, r: (i, r, 0)),
            gain,
            pl.BlockSpec((None, d, MEM_WIDTH), lambda i, r: (layer, 0, 0)),
            mem_spec, mem_spec,
            pl.BlockSpec((None, MEM_WIDTH, d), lambda i, r: (layer, 0, 0)),
            gain,
        ],
        out_specs=pl.BlockSpec((nb, tm, d), lambda i, r: (i, r, 0)),
        out_shape=jax.ShapeDtypeStruct((b, seq, d), F32),
        compiler_params=_params(2),
        name="mem_attn",
    )(x3, g_pre, w_mq, mk, mv, w_mo, g_post)


def _pad_front(hist, rows):
    return jnp.pad(hist, ((0, 0), (rows - hist.shape[1], 0), (0, 0)))


def kernel(x_prompt, x_sample, mem_prompt, cache_diff_k, cache_diff_v, state_pool, state_conv, cache_mem_k, cache_mem_v, ffn1_pre_g, ffn1_post_g, ffn1_wg, ffn1_wu, ffn1_wd, mix_pre_g, mix_post_g, w_in, pool_w, pool_scale, diff_lq1, diff_lk1, diff_lq2, diff_lk2, diff_subln_g, conv_w, w_out, mem_pre_g, mem_post_g, mem_kv_g, w_mq, w_mk, w_mv, w_mo, ffn2_pre_g, ffn2_post_g, ffn2_wg, ffn2_wu, ffn2_wd):
    n_layers = w_in.shape[0]
    bp, sp, d = x_prompt.shape
    bs, ss, _ = x_sample.shape
    past = cache_diff_k.shape[2]
    mem_len = mem_prompt.shape[1]
    assert sp >= POOL_HIST and ss >= POOL_HIST

    bf = lambda w: w.astype(BF16)
    row = lambda v: v.reshape(n_layers, 1, v.shape[-1])
    ffn1 = (row(ffn1_pre_g), bf(ffn1_wg), bf(ffn1_wu), bf(ffn1_wd), row(ffn1_post_g))
    ffn2 = (row(ffn2_pre_g), bf(ffn2_wg), bf(ffn2_wu), bf(ffn2_wd), row(ffn2_post_g))
    mix_pre_g, mix_post_g = row(mix_pre_g), row(mix_post_g)
    mem_pre_g, mem_post_g, mem_kv_g = row(mem_pre_g), row(mem_post_g), row(mem_kv_g)
    w_in, w_out, w_mq, w_mk, w_mv, w_mo, pool_w = map(bf, (_permute_in_proj(w_in), w_out, w_mq, w_mk, w_mv, w_mo, pool_w))
    pool_scale = row(pool_scale)
    lam_vecs = tuple(row(v) for v in (diff_lq1, diff_lk1, diff_lq2, diff_lk2, diff_subln_g))
    slopes = jnp.asarray([2.0 ** (-8.0 * (h + 1) / N_DIFF_HEADS) for h in range(N_DIFF_HEADS)], F32)

    mk_p, mv_p = _mem_kv(mem_prompt.reshape(bp * mem_len, d), mem_kv_g, w_mk, w_mv, tm=1024)
    mk_p = mk_p.reshape(n_layers, bp, mem_len, MEM_WIDTH)
    mv_p = mv_p.reshape(n_layers, bp, mem_len, MEM_WIDTH)
    mk_s = cache_mem_k.reshape(n_layers, bs, mem_len, MEM_WIDTH)
    mv_s = cache_mem_v.reshape(n_layers, bs, mem_len, MEM_WIDTH)

    zero_pool = jnp.zeros((bp, POOL_HIST_PAD, POOL_WIDTH), F32)
    zero_conv = jnp.zeros((bp, CONV_HIST_PAD, CONV_WIDTH), F32)

    xp = x_prompt.reshape(bp * sp, d)
    xs = x_sample.reshape(bs * ss, d)
    small = {name: [] for name in ("pp", "cp", "ps", "cs")}
    stacks = {"p": (None, None), "s": (None, None)}

    def layer_step(x, b, seq, l, group, hist_pool, hist_conv, pos0, mk, mv, mem_tm):
        lam_init = 0.8 - 0.6 * math.exp(-0.3 * l)
        prompt = group == "p"
        x = _ffn(x, *ffn1, l, tm=512, tf=512)
        proj = _in_proj(x, mix_pre_g, w_in, l, *stacks[group], n_layers=n_layers, seq=seq, tm=512, head_major=prompt)
        q, k_stack, v_stack = proj[:3]
        stacks[group] = (k_stack, v_stack)
        zp, cb, cc, cx = (a.reshape(b, seq, -1) for a in proj[-4:])
        y_pool, y_conv, new_conv = _pool_conv(zp, cb, cc, cx, hist_pool, hist_conv, pool_w, pool_scale, conv_w, l,
                                              pos0=pos0)
        q3 = q.reshape(b, seq, DIFF_WIDTH)
        if prompt:
            y_diff = _diff_attn_prompt(q3, proj[3], proj[4], slopes, *lam_vecs, l, tq=512, tk=256, lam_init=lam_init)
        else:
            rows5 = (n_layers, b, seq, N_DIFF_HEADS, DIFF_V_DIM)
            y_diff = _diff_attn_sample(q3, k_stack.reshape(rows5), v_stack.reshape(rows5), cache_diff_k, cache_diff_v,
                                       slopes, *lam_vecs, l, tk=min(1024, past), lam_init=lam_init)
        x = _out_proj(y_pool.reshape(b * seq, -1), y_diff.reshape(b * seq, -1), y_conv.reshape(b * seq, -1),
                      w_out, x, mix_post_g, l, tm=512)
        x = _mem_attn(x.reshape(b, seq, d), mem_pre_g, w_mq, mk, mv, w_mo, mem_post_g, l, l, tm=mem_tm)
        x = _ffn(x.reshape(b * seq, d), *ffn2, l, tm=512, tf=512)
        return x, zp[:, seq - POOL_HIST:, :], new_conv

    for l in range(n_layers):
        xp, pp, cp = layer_step(xp, bp, sp, l, "p", zero_pool, zero_conv, 0, mk_p, mv_p, 512)
        xs, ps, cs = layer_step(xs, bs, ss, l, "s", _pad_front(state_pool[l], POOL_HIST_PAD),
                                _pad_front(state_conv[l], CONV_HIST_PAD), past, mk_s, mv_s, ss)
        for name, val in zip(("pp", "cp", "ps", "cs"), (pp, cp, ps, cs)):
            small[name].append(val)

    st = lambda name: jnp.stack(small[name])
    rows_p = (n_layers, bp, sp, N_DIFF_HEADS, DIFF_V_DIM)
    rows_s = (n_layers, bs, ss, N_DIFF_HEADS, DIFF_V_DIM)
    head_shape = (n_layers, bp, mem_len, N_MEM_HEADS, MEM_HEAD_DIM)
    return (xp.reshape(bp, sp, d), xs.reshape(bs, ss, d),
            stacks["p"][0].reshape(rows_p), stacks["p"][1].reshape(rows_p), st("pp"), st("cp"),
            mk_p.reshape(head_shape), mv_p.reshape(head_shape),
            stacks["s"][0].reshape(rows_s), stacks["s"][1].reshape(rows_s), st("ps"), st("cs"))
```

```python
import functools
import math

import jax
import jax.numpy as jnp
from jax import lax
from jax.experimental import pallas as pl
from jax.experimental.pallas import tpu as pltpu

F32 = jnp.float32
BF16 = jnp.bfloat16

EPS = 1e-6
CHUNK = 64
POOL_WINDOWS = (2, 4, 8, 16)
POOL_GROUP_DIM = 128
POOL_WIDTH = POOL_GROUP_DIM * len(POOL_WINDOWS)
POOL_HIST = max(POOL_WINDOWS) - 1
POOL_HIST_PAD = 16
N_DIFF_HEADS = 4
DIFF_HEAD_DIM = 128
DIFF_V_DIM = 2 * DIFF_HEAD_DIM
DIFF_WIDTH = N_DIFF_HEADS * DIFF_V_DIM
CONV_WIDTH = 512
CONV_K = 3
CONV_HIST = CONV_K - 1
CONV_HIST_PAD = 8
N_MEM_HEADS = 4
MEM_HEAD_DIM = 128
MEM_WIDTH = N_MEM_HEADS * MEM_HEAD_DIM
MEM_ROWS_PER_STEP = 512
OFF_Q = POOL_WIDTH
OFF_K = OFF_Q + DIFF_WIDTH
OFF_V = OFF_K + DIFF_WIDTH
OFF_CB = OFF_V + DIFF_WIDTH
OFF_CC = OFF_CB + CONV_WIDTH
OFF_CX = OFF_CC + CONV_WIDTH

V7X_VMEM_LIMIT_BYTES = 56 * 1024 * 1024


def _params(n_axes):
    return pltpu.CompilerParams(dimension_semantics=("arbitrary",) * n_axes,
                                vmem_limit_bytes=V7X_VMEM_LIMIT_BYTES)


def _rms(xf, g):
    ms = jnp.mean(xf * xf, axis=-1, keepdims=True)
    return xf * lax.rsqrt(ms + EPS) * g


def _dot(a, b):
    return jnp.dot(a, b, preferred_element_type=F32)


def _dot_nt(a, b):
    return lax.dot_general(a, b, (((1,), (1,)), ((), ())), preferred_element_type=F32)


def _ffn_kernel(x_ref, gpre_ref, wg_ref, wu_ref, wd_ref, gpost_ref, y_ref, xn_ref, *, n_ff_steps):
    j = pl.program_id(1)

    @pl.when(j == 0)
    def _():
        xn_ref[...] = _rms(x_ref[...], gpre_ref[...]).astype(BF16)
        y_ref[...] = jnp.zeros(y_ref.shape, F32)

    xn = xn_ref[...]
    gate = _dot(xn, wg_ref[...])
    up = _dot(xn, wu_ref[...])
    h = (gate * jax.nn.sigmoid(gate)) * up
    y_ref[...] += _dot(h.astype(BF16), wd_ref[...])

    @pl.when(j == n_ff_steps - 1)
    def _():
        y_ref[...] = x_ref[...] + 0.5 * _rms(y_ref[...], gpost_ref[...])


def _ffn(x, g_pre, wg, wu, wd, g_post, layer, *, tm, tf):
    n, d = x.shape
    f = wg.shape[-1]
    tm, tf = min(tm, n), min(tf, f)
    assert n % tm == 0 and f % tf == 0
    return pl.pallas_call(
        functools.partial(_ffn_kernel, n_ff_steps=f // tf),
        grid=(n // tm, f // tf),
        in_specs=[
            pl.BlockSpec((tm, d), lambda i, j: (i, 0)),
            pl.BlockSpec((None, 1, d), lambda i, j: (layer, 0, 0)),
            pl.BlockSpec((None, d, tf), lambda i, j: (layer, 0, j)),
            pl.BlockSpec((None, d, tf), lambda i, j: (layer, 0, j)),
            pl.BlockSpec((None, tf, d), lambda i, j: (layer, j, 0)),
            pl.BlockSpec((None, 1, d), lambda i, j: (layer, 0, 0)),
        ],
        out_specs=pl.BlockSpec((tm, d), lambda i, j: (i, 0)),
        out_shape=jax.ShapeDtypeStruct((n, d), F32),
        scratch_shapes=[pltpu.VMEM((tm, d), BF16)],
        compiler_params=_params(2),
        name="ffn",
    )(x, g_pre, wg, wu, wd, g_post)


IN_PROJ_BLOCK = 1024
IN_PROJ_ORDER = ((OFF_Q, OFF_K), (OFF_K, OFF_V), (OFF_V, OFF_CB), (0, OFF_Q), (OFF_CB, OFF_CC), (OFF_CC, OFF_CX),
                 (OFF_CX, OFF_CX + CONV_WIDTH))


def _permute_in_proj(w_in):
    return jnp.concatenate([w_in[..., lo:hi] for lo, hi in IN_PROJ_ORDER], axis=-1)


def _store_heads(z_ref, rows_ref, heads_ref):
    heads = jnp.stack([z_ref[:, h * DIFF_V_DIM:(h + 1) * DIFF_V_DIM] for h in range(N_DIFF_HEADS)])
    rows_ref[...] = jnp.swapaxes(heads, 0, 1)
    if heads_ref is not None:
        heads_ref[...] = heads.astype(BF16)


def _in_proj_kernel(*refs, head_major, aliased):
    refs = list(refs)
    x_ref, g_ref, w_ref = refs[:3]
    refs = refs[3 + (2 if aliased else 0):]
    q_ref, kf_ref, vf_ref = refs[:3]
    refs = refs[3:]
    kb_ref, vb_ref = (refs[0], refs[1]) if head_major else (None, None)
    refs = refs[2 if head_major else 0:]
    zp_ref, cb_ref, cc_ref, cx_ref, xn_ref, zk_ref, zv_ref = refs
    j = pl.program_id(1)
    half = IN_PROJ_BLOCK // 2

    @pl.when(j == 0)
    def _():
        xn_ref[...] = _rms(x_ref[...], g_ref[...]).astype(BF16)
        q_ref[...] = _dot(xn_ref[...], w_ref[...]).astype(BF16)

    @pl.when(j == 1)
    def _():
        zk_ref[...] = _dot(xn_ref[...], w_ref[...])

    @pl.when(j == 2)
    def _():
        zv_ref[...] = _dot(xn_ref[...], w_ref[...])
        _store_heads(zk_ref, kf_ref, kb_ref)

    @pl.when(j == 3)
    def _():
        z = _dot(xn_ref[...], w_ref[...])
        zp_ref[...] = z[:, :half]
        cb_ref[...] = z[:, half:]
        _store_heads(zv_ref, vf_ref, vb_ref)

    @pl.when(j == 4)
    def _():
        z = _dot(xn_ref[...], w_ref[...])
        cc_ref[...] = z[:, :half]
        cx_ref[...] = z[:, half:]


def _in_proj(x, g, w_perm, layer, k_stack, v_stack, *, n_layers, seq, tm, head_major):
    n, d = x.shape
    tm = min(tm, n, seq) if head_major else min(tm, n)
    assert n % tm == 0 and w_perm.shape[-1] == 5 * IN_PROJ_BLOCK and (not head_major or seq % tm == 0)
    aliased = k_stack is not None
    rows = lambda width, dt: (pl.BlockSpec((tm, width), lambda i, j: (i, 0)), jax.ShapeDtypeStruct((n, width), dt))
    stack = (pl.BlockSpec((None, tm, N_DIFF_HEADS, DIFF_V_DIM), lambda i, j: (layer, i, 0, 0)),
             jax.ShapeDtypeStruct((n_layers, n, N_DIFF_HEADS, DIFF_V_DIM), F32))
    tiles = seq // tm if head_major else 1
    heads = (pl.BlockSpec((None, N_DIFF_HEADS, tm, DIFF_V_DIM), lambda i, j: (i // tiles, 0, i % tiles, 0)),
             jax.ShapeDtypeStruct((n // seq, N_DIFF_HEADS, seq, DIFF_V_DIM), BF16))
    outs = [rows(DIFF_WIDTH, BF16), stack, stack] + ([heads, heads] if head_major else []) + [rows(512, F32)] * 4
    in_specs = [
        pl.BlockSpec((tm, d), lambda i, j: (i, 0)),
        pl.BlockSpec((None, 1, d), lambda i, j: (layer, 0, 0)),
        pl.BlockSpec((None, d, IN_PROJ_BLOCK), lambda i, j: (layer, 0, j)),
    ]
    args = [x, g, w_perm]
    if aliased:
        in_specs += [pl.BlockSpec(memory_space=pl.ANY)] * 2
        args += [k_stack, v_stack]
    return pl.pallas_call(
        functools.partial(_in_proj_kernel, head_major=head_major, aliased=aliased),
        grid=(n // tm, 5),
        in_specs=in_specs,
        out_specs=[o[0] for o in outs],
        out_shape=[o[1] for o in outs],
        input_output_aliases={3: 1, 4: 2} if aliased else {},
        scratch_shapes=[pltpu.VMEM((tm, d), BF16), pltpu.VMEM((tm, DIFF_WIDTH), F32), pltpu.VMEM((tm, DIFF_WIDTH), F32)],
        compiler_params=_params(2),
        name="in_proj",
    )(*args)


def _mem_kv_kernel(x_ref, g_ref, wk_ref, wv_ref, k_ref, v_ref):
    xn = _rms(x_ref[...], g_ref[...]).astype(BF16)
    k_ref[...] = _dot(xn, wk_ref[...])
    v_ref[...] = _dot(xn, wv_ref[...])


def _mem_kv(mem, g, wk, wv, *, tm):
    r, d = mem.shape
    n_layers = wk.shape[0]
    tm = min(tm, r)
    assert r % tm == 0
    out = jax.ShapeDtypeStruct((n_layers, r, MEM_WIDTH), F32)
    return pl.pallas_call(
        _mem_kv_kernel,
        grid=(n_layers, r // tm),
        in_specs=[
            pl.BlockSpec((tm, d), lambda l, i: (i, 0)),
            pl.BlockSpec((None, 1, d), lambda l, i: (l, 0, 0)),
            pl.BlockSpec((None, d, MEM_WIDTH), lambda l, i: (l, 0, 0)),
            pl.BlockSpec((None, d, MEM_WIDTH), lambda l, i: (l, 0, 0)),
        ],
        out_specs=[pl.BlockSpec((None, tm, MEM_WIDTH), lambda l, i: (l, i, 0))] * 2,
        out_shape=[out, out],
        compiler_params=_params(2),
        name="mem_kv",
    )(mem, g, wk, wv)


def _pool_conv_kernel(zp_ref, cb_ref, cc_ref, cx_ref, hp_ref, hc_ref, pw_ref, ps_ref, cw_ref,
                      yp_ref, yc_ref, nc_ref, extp_ref, extc_ref, *, seq, pos0, rows):
    extp_ref[0:POOL_HIST_PAD, :] = hp_ref[...]
    extp_ref[POOL_HIST_PAD:POOL_HIST_PAD + seq, :] = zp_ref[...]
    extc_ref[0:CONV_HIST_PAD, :] = hc_ref[...]
    extc_ref[CONV_HIST_PAD:CONV_HIST_PAD + seq, :] = cc_ref[...] * cx_ref[...]
    nc_ref[...] = extc_ref[CONV_HIST_PAD + seq - CONV_HIST:CONV_HIST_PAD + seq, :]

    for r0 in range(0, seq, rows):
        pos = pos0 + r0 + lax.broadcasted_iota(jnp.int32, (rows, 1), 0)
        for g, win in enumerate(POOL_WINDOWS):
            c = slice(g * POOL_GROUP_DIM, (g + 1) * POOL_GROUP_DIM)
            base = POOL_HIST_PAD + r0
            tok = extp_ref[base:base + rows, c]
            wsum = tok
            for k in range(1, win):
                wsum = wsum + extp_ref[base - k:base - k + rows, c]
            cnt = jnp.minimum(pos + 1, win).astype(F32)
            pooled = wsum / cnt - tok
            mixed = _dot(pooled.astype(BF16), pw_ref[g])
            yp_ref[r0:r0 + rows, c] = (mixed * ps_ref[:, c]).astype(BF16)
        base = CONV_HIST_PAD + r0
        y = cw_ref[0:1, :] * extc_ref[base - 2:base - 2 + rows, :]
        y = y + cw_ref[1:2, :] * extc_ref[base - 1:base - 1 + rows, :]
        y = y + cw_ref[2:3, :] * extc_ref[base:base + rows, :]
        yc_ref[r0:r0 + rows, :] = (cb_ref[r0:r0 + rows, :] * y).astype(BF16)


def _pool_conv(zp, cb, cc, cx, hist_pool, hist_conv, pool_w, pool_scale, conv_w, layer, *, pos0):
    b, seq, _ = zp.shape
    assert seq >= CONV_HIST and seq % 8 == 0
    rows = min(seq, 256)
    assert seq % rows == 0
    seq_block = pl.BlockSpec((None, seq, 512), lambda i: (i, 0, 0))
    return pl.pallas_call(
        functools.partial(_pool_conv_kernel, seq=seq, pos0=pos0, rows=rows),
        grid=(b,),
        in_specs=[
            seq_block, seq_block, seq_block, seq_block,
            pl.BlockSpec((None, POOL_HIST_PAD, POOL_WIDTH), lambda i: (i, 0, 0)),
            pl.BlockSpec((None, CONV_HIST_PAD, CONV_WIDTH), lambda i: (i, 0, 0)),
            pl.BlockSpec((None, len(POOL_WINDOWS), POOL_GROUP_DIM, POOL_GROUP_DIM),
                         lambda i: (layer, 0, 0, 0)),
            pl.BlockSpec((None, 1, POOL_WIDTH), lambda i: (layer, 0, 0)),
            pl.BlockSpec((None, CONV_K, CONV_WIDTH), lambda i: (layer, 0, 0)),
        ],
        out_specs=[
            pl.BlockSpec((None, seq, POOL_WIDTH), lambda i: (i, 0, 0)),
            pl.BlockSpec((None, seq, CONV_WIDTH), lambda i: (i, 0, 0)),
            pl.BlockSpec((None, CONV_HIST, CONV_WIDTH), lambda i: (i, 0, 0)),
        ],
        out_shape=[
            jax.ShapeDtypeStruct((b, seq, POOL_WIDTH), BF16),
            jax.ShapeDtypeStruct((b, seq, CONV_WIDTH), BF16),
            jax.ShapeDtypeStruct((b, CONV_HIST, CONV_WIDTH), F32),
        ],
        scratch_shapes=[pltpu.VMEM((POOL_HIST_PAD + seq, POOL_WIDTH), F32),
                        pltpu.VMEM((CONV_HIST_PAD + seq, CONV_WIDTH), F32)],
        compiler_params=_params(1),
        name="pool_conv",
    )(zp, cb, cc, cx, hist_pool, hist_conv, pool_w, pool_scale, conv_w)


def _diff_lambda(lq1_ref, lk1_ref, lq2_ref, lk2_ref, lam_init):
    a = jnp.sum(lq1_ref[...] * lk1_ref[...], axis=-1, keepdims=True)
    b = jnp.sum(lq2_ref[...] * lk2_ref[...], axis=-1, keepdims=True)
    return jnp.exp(a) - jnp.exp(b) + lam_init


def _diff_prompt_kernel(slopes_ref, q_ref, k_ref, v_ref, lq1_ref, lk1_ref, lq2_ref, lk2_ref, g_ref,
                        o_ref, bm_ref, s_ref, mx_ref, l_ref, acc_ref, *, tq, tk, lam_init):
    h = pl.program_id(1)
    qi = pl.program_id(2)
    n_diag = tq // tk

    @pl.when((pl.program_id(0) == 0) & (h == 0) & (qi == 0))
    def _():
        ii = lax.broadcasted_iota(jnp.int32, (tq, tk), 0)
        jj = lax.broadcasted_iota(jnp.int32, (tq, tk), 1)
        for d in range(n_diag):
            kj = jj + d * tk
            base = (ii - jnp.abs(ii - kj)).astype(F32)
            bm_ref[d] = jnp.where(kj <= (ii | (CHUNK - 1)), base, -jnp.inf)

    qs = (q_ref[:, :DIFF_HEAD_DIM], q_ref[:, DIFF_HEAD_DIM:])
    scale = DIFF_HEAD_DIM ** -0.5
    slope = slopes_ref[h]
    col = lax.broadcasted_iota(jnp.int32, (1, tk), 1)

    def tile_index(kt, d):
        return kt if d is None else qi * n_diag + d

    def key_start(kt, d):
        return pl.multiple_of(tile_index(kt, d) * tk, tk)

    def scores(kt, c, d):
        k0 = key_start(kt, d)
        kk = k_ref[pl.ds(k0, tk), c * DIFF_HEAD_DIM:(c + 1) * DIFF_HEAD_DIM]
        s = _dot_nt(qs[c], kk) * scale
        if d is None:
            return s + slope * (k0 - qi * tq + col).astype(F32)
        return s + slope * bm_ref[d]

    def earlier_tiles(fn):
        def body(i, carry):
            for u in range(n_diag):
                fn(i * n_diag + u, None)
            return carry
        lax.fori_loop(0, qi, body, 0)

    def pass1(kt, d):
        for c in range(2):
            s = scores(kt, c, d)
            s_ref[c, tile_index(kt, d)] = s
            mx_ref[c] = s if d == 0 else jnp.maximum(mx_ref[c], s)

    for d in range(n_diag):
        pass1(None, d)
    earlier_tiles(pass1)
    m_full = [jnp.broadcast_to(jnp.max(mx_ref[c], axis=-1, keepdims=True), (tq, tk)) for c in range(2)]

    def pass2(kt, d):
        vv = v_ref[pl.ds(key_start(kt, d), tk), :]
        for c in range(2):
            p = jnp.exp(s_ref[c, tile_index(kt, d)] - m_full[c])
            pv = _dot(p.astype(BF16), vv)
            if d == 0:
                l_ref[c] = p
                acc_ref[c] = pv
            else:
                l_ref[c] += p
                acc_ref[c] += pv

    for d in range(n_diag):
        pass2(None, d)
    earlier_tiles(pass2)

    lam = _diff_lambda(lq1_ref, lk1_ref, lq2_ref, lk2_ref, lam_init)
    l1 = jnp.sum(l_ref[0], axis=-1, keepdims=True)
    l2 = jnp.sum(l_ref[1], axis=-1, keepdims=True)
    o = acc_ref[0] / l1 - lam * (acc_ref[1] / l2)
    o_ref[...] = (_rms(o, g_ref[...]) * (1.0 - lam_init)).astype(BF16)


def _diff_lambda_specs(layer, n_axes):
    idx = {2: lambda a, b: (layer, 0, 0), 3: lambda a, b, c: (layer, 0, 0)}[n_axes]
    return [pl.BlockSpec((None, 1, DIFF_HEAD_DIM), idx)] * 4 + [pl.BlockSpec((None, 1, DIFF_V_DIM), idx)]


def _diff_attn_prompt(q3, kh, vh, slopes, lq1, lk1, lq2, lk2, subln_g, layer, *, tq, tk, lam_init):
    b, seq, _ = q3.shape
    tq = min(tq, seq)
    assert seq % tq == 0 and tq % tk == 0 and tk % CHUNK == 0 and CHUNK & (CHUNK - 1) == 0
    return pl.pallas_call(
        functools.partial(_diff_prompt_kernel, tq=tq, tk=tk, lam_init=lam_init),
        grid=(b, N_DIFF_HEADS, seq // tq),
        in_specs=[
            pl.BlockSpec(memory_space=pltpu.SMEM),
            pl.BlockSpec((None, tq, DIFF_V_DIM), lambda i, h, q: (i, q, h)),
            pl.BlockSpec((None, None, seq, DIFF_V_DIM), lambda i, h, q: (i, h, 0, 0)),
            pl.BlockSpec((None, None, seq, DIFF_V_DIM), lambda i, h, q: (i, h, 0, 0)),
        ] + _diff_lambda_specs(layer, 3),
        out_specs=pl.BlockSpec((None, tq, DIFF_V_DIM), lambda i, h, q: (i, q, h)),
        out_shape=jax.ShapeDtypeStruct((b, seq, DIFF_WIDTH), BF16),
        scratch_shapes=[
            pltpu.VMEM((tq // tk, tq, tk), F32),
            pltpu.VMEM((2, seq // tk, tq, tk), F32),
            pltpu.VMEM((2, tq, tk), F32),
            pltpu.VMEM((2, tq, tk), F32),
            pltpu.VMEM((2, tq, DIFF_V_DIM), F32),
        ],
        compiler_params=_params(3),
        name="diff_attn_prompt",
    )(slopes, q3, kh, vh, lq1, lk1, lq2, lk2, subln_g)


def _diff_sample_kernel(slopes_ref, q_ref, kn_ref, vn_ref, kc_ref, vc_ref,
                        lq1_ref, lk1_ref, lq2_ref, lk2_ref, g_ref,
                        o_ref, m_ref, l_ref, acc_ref, *, past, seq, tk, lam_init):
    kt = pl.program_id(1)
    n_cache_tiles = past // tk

    scale = DIFF_HEAD_DIM ** -0.5

    def process(k_heads, v_heads, n_keys, new_keys):
        ii = lax.broadcasted_iota(jnp.int32, (seq, n_keys), 0)
        jj = lax.broadcasted_iota(jnp.int32, (seq, n_keys), 1)
        if new_keys:
            dist = jnp.abs(ii - jj)
        else:
            dist = ii - jj + (past - kt * tk)
        dist = dist.astype(F32)
        s_all = []
        for h in range(N_DIFF_HEADS):
            bias = -slopes_ref[h] * dist
            for c in range(2):
                lo = h * DIFF_V_DIM + c * DIFF_HEAD_DIM
                qc = q_ref[:, lo:lo + DIFF_HEAD_DIM]
                kc = k_heads[h][:, c * DIFF_HEAD_DIM:(c + 1) * DIFF_HEAD_DIM]
                s_all.append(_dot_nt(qc, kc) * scale + bias)
        s = jnp.stack(s_all)
        m_old = m_ref[...]
        m_new = jnp.maximum(m_old, jnp.max(s, axis=-1, keepdims=True))
        alpha = jnp.exp(m_old - m_new)
        p = jnp.exp(s - m_new)
        l_ref[...] = alpha * l_ref[...] + jnp.sum(p, axis=-1, keepdims=True)
        pb = p.astype(BF16)
        pv = jnp.stack([_dot(pb[i], v_heads[i // 2]) for i in range(2 * N_DIFF_HEADS)])
        acc_ref[...] = alpha * acc_ref[...] + pv
        m_ref[...] = m_new

    def heads_first(ref):
        x = jnp.swapaxes(ref[...], 0, 1).astype(BF16)
        return [x[h] for h in range(N_DIFF_HEADS)]

    @pl.when(kt == 0)
    def _():
        m_ref[...] = jnp.full(m_ref.shape, -jnp.inf, F32)
        l_ref[...] = jnp.zeros(l_ref.shape, F32)
        acc_ref[...] = jnp.zeros(acc_ref.shape, F32)
        process(heads_first(kn_ref), heads_first(vn_ref), seq, True)

    process(heads_first(kc_ref), heads_first(vc_ref), tk, False)

    @pl.when(kt == n_cache_tiles - 1)
    def _():
        lam = _diff_lambda(lq1_ref, lk1_ref, lq2_ref, lk2_ref, lam_init)
        for h in range(N_DIFF_HEADS):
            o = acc_ref[2 * h] / l_ref[2 * h] - lam * (acc_ref[2 * h + 1] / l_ref[2 * h + 1])
            y = _rms(o, g_ref[...]) * (1.0 - lam_init)
            o_ref[:, h * DIFF_V_DIM:(h + 1) * DIFF_V_DIM] = y.astype(BF16)


def _diff_attn_sample(q3, k_new, v_new, cache_k, cache_v, slopes, lq1, lk1, lq2, lk2, subln_g, layer, *, tk, lam_init):
    b, seq, _ = q3.shape
    past = cache_k.shape[2]
    assert past % tk == 0 and past % CHUNK == 0 and seq <= CHUNK
    n_cache_tiles = past // tk
    cache_spec = pl.BlockSpec((None, None, tk, N_DIFF_HEADS, DIFF_V_DIM), lambda i, k: (layer, i, k, 0, 0))
    q_spec = pl.BlockSpec((None, seq, DIFF_WIDTH), lambda i, k: (i, 0, 0))
    new_spec = pl.BlockSpec((None, None, seq, N_DIFF_HEADS, DIFF_V_DIM), lambda i, k: (layer, i, 0, 0, 0))
    return pl.pallas_call(
        functools.partial(_diff_sample_kernel, past=past, seq=seq, tk=tk, lam_init=lam_init),
        grid=(b, n_cache_tiles),
        in_specs=[
            pl.BlockSpec(memory_space=pltpu.SMEM),
            q_spec, new_spec, new_spec,
            cache_spec, cache_spec,
        ] + _diff_lambda_specs(layer, 2),
        out_specs=pl.BlockSpec((None, seq, DIFF_WIDTH), lambda i, k: (i, 0, 0)),
        out_shape=jax.ShapeDtypeStruct((b, seq, DIFF_WIDTH), BF16),
        scratch_shapes=[
            pltpu.VMEM((2 * N_DIFF_HEADS, seq, 1), F32),
            pltpu.VMEM((2 * N_DIFF_HEADS, seq, 1), F32),
            pltpu.VMEM((2 * N_DIFF_HEADS, seq, DIFF_V_DIM), F32),
        ],
        compiler_params=_params(2),
        name="diff_attn_sample",
    )(slopes, q3, k_new, v_new, cache_k, cache_v, lq1, lk1, lq2, lk2, subln_g)


def _out_proj_kernel(yp_ref, yd_ref, yc_ref, w_ref, x_ref, g_ref, o_ref):
    m = _dot(yp_ref[...], w_ref[0:POOL_WIDTH, :])
    m = m + _dot(yd_ref[...], w_ref[POOL_WIDTH:POOL_WIDTH + DIFF_WIDTH, :])
    m = m + _dot(yc_ref[...], w_ref[POOL_WIDTH + DIFF_WIDTH:, :])
    o_ref[...] = x_ref[...] + _rms(m, g_ref[...])


def _out_proj(yp, yd, yc, w_out, x, g_post, layer, *, tm):
    n, d = x.shape
    tm = min(tm, n)
    assert n % tm == 0
    row = lambda width: pl.BlockSpec((tm, width), lambda i: (i, 0))
    return pl.pallas_call(
        _out_proj_kernel,
        grid=(n // tm,),
        in_specs=[
            row(POOL_WIDTH), row(DIFF_WIDTH), row(CONV_WIDTH),
            pl.BlockSpec((None, w_out.shape[1], d), lambda i: (layer, 0, 0)),
            row(d),
            pl.BlockSpec((None, 1, d), lambda i: (layer, 0, 0)),
        ],
        out_specs=row(d),
        out_shape=jax.ShapeDtypeStruct((n, d), F32),
        compiler_params=_params(1),
        name="out_proj",
    )(yp, yd, yc, w_out, x, g_post)


def _mem_attn_kernel(x_ref, gpre_ref, wq_ref, mk_ref, mv_ref, wo_ref, gpost_ref, o_ref):
    nb, tm, d = x_ref.shape
    x = x_ref[...].reshape(nb * tm, d)
    q = _dot(_rms(x, gpre_ref[...]).astype(BF16), wq_ref[...]).astype(BF16)
    scale = MEM_HEAD_DIM ** -0.5
    rows = []
    for i in range(nb):
        heads = []
        for h in range(N_MEM_HEADS):
            c = slice(h * MEM_HEAD_DIM, (h + 1) * MEM_HEAD_DIM)
            s = _dot_nt(q[i * tm:(i + 1) * tm, c], mk_ref[i, :, c].astype(BF16)) * scale
            p = jnp.exp(s - jnp.max(s, axis=-1, keepdims=True))
            p = p / jnp.sum(p, axis=-1, keepdims=True)
            heads.append(_dot(p.astype(BF16), mv_ref[i, :, c].astype(BF16)))
        rows.append(jnp.concatenate(heads, axis=-1))
    o = jnp.concatenate(rows, axis=0).astype(BF16)
    y = x + _rms(_dot(o, wo_ref[...]), gpost_ref[...])
    o_ref[...] = y.reshape(nb, tm, d)


def _mem_attn(x3, g_pre, w_mq, mk, mv, w_mo, g_post, layer, mem_layer, *, tm):
    b, seq, d = x3.shape
    mem_len = mk.shape[2]
    tm = min(tm, seq)
    nb = math.gcd(b, max(1, MEM_ROWS_PER_STEP // tm))
    assert seq % tm == 0 and tm % 8 == 0
    mem_spec = pl.BlockSpec((None, nb, mem_len, MEM_WIDTH), lambda i, r: (mem_layer, i, 0, 0))
    gain = pl.BlockSpec((None, 1, d), lambda i, r: (layer, 0, 0))
    return pl.pallas_call(
        _mem_attn_kernel,
        grid=(b // nb, seq // tm),
        in_specs=[
            pl.BlockSpec((nb, tm, d), lambda i, r: (i, r, 0)),
            gain,
            pl.BlockSpec((None, d, MEM_WIDTH), lambda i, r: (layer, 0, 0)),
            mem_spec, mem_spec,
            pl.BlockSpec((None, MEM_WIDTH, d), lambda i, r: (layer, 0, 0)),
            gain,
        ],
        out_specs=pl.BlockSpec((nb, tm, d), lambda i, r: (i, r, 0)),
        out_shape=jax.ShapeDtypeStruct((b, seq, d), F32),
        compiler_params=_params(2),
        name="mem_attn",
    )(x3, g_pre, w_mq, mk, mv, w_mo, g_post)


def _pad_front(hist, rows):
    return jnp.pad(hist, ((0, 0), (rows - hist.shape[1], 0), (0, 0)))


def kernel(x_prompt, x_sample, mem_prompt, cache_diff_k, cache_diff_v, state_pool, state_conv, cache_mem_k, cache_mem_v, ffn1_pre_g, ffn1_post_g, ffn1_wg, ffn1_wu, ffn1_wd, mix_pre_g, mix_post_g, w_in, pool_w, pool_scale, diff_lq1, diff_lk1, diff_lq2, diff_lk2, diff_subln_g, conv_w, w_out, mem_pre_g, mem_post_g, mem_kv_g, w_mq, w_mk, w_mv, w_mo, ffn2_pre_g, ffn2_post_g, ffn2_wg, ffn2_wu, ffn2_wd):
    n_layers = w_in.shape[0]
    bp, sp, d = x_prompt.shape
    bs, ss, _ = x_sample.shape
    past = cache_diff_k.shape[2]
    mem_len = mem_prompt.shape[1]
    assert sp >= POOL_HIST and ss >= POOL_HIST

    bf = lambda w: w.astype(BF16)
    row = lambda v: v.reshape(n_layers, 1, v.shape[-1])
    ffn1 = (row(ffn1_pre_g), bf(ffn1_wg), bf(ffn1_wu), bf(ffn1_wd), row(ffn1_post_g))
    ffn2 = (row(ffn2_pre_g), bf(ffn2_wg), bf(ffn2_wu), bf(ffn2_wd), row(ffn2_post_g))
    mix_pre_g, mix_post_g = row(mix_pre_g), row(mix_post_g)
    mem_pre_g, mem_post_g, mem_kv_g = row(mem_pre_g), row(mem_post_g), row(mem_kv_g)
    w_in, w_out, w_mq, w_mk, w_mv, w_mo, pool_w = map(bf, (_permute_in_proj(w_in), w_out, w_mq, w_mk, w_mv, w_mo, pool_w))
    pool_scale = row(pool_scale)
    lam_vecs = tuple(row(v) for v in (diff_lq1, diff_lk1, diff_lq2, diff_lk2, diff_subln_g))
    slopes = jnp.asarray([2.0 ** (-8.0 * (h + 1) / N_DIFF_HEADS) for h in range(N_DIFF_HEADS)], F32)

    mk_p, mv_p = _mem_kv(mem_prompt.reshape(bp * mem_len, d), mem_kv_g, w_mk, w_mv, tm=1024)
    mk_p = mk_p.reshape(n_layers, bp, mem_len, MEM_WIDTH)
    mv_p = mv_p.reshape(n_layers, bp, mem_len, MEM_WIDTH)
    mk_s = cache_mem_k.reshape(n_layers, bs, mem_len, MEM_WIDTH)
    mv_s = cache_mem_v.reshape(n_layers, bs, mem_len, MEM_WIDTH)

    zero_pool = jnp.zeros((bp, POOL_HIST_PAD, POOL_WIDTH), F32)
    zero_conv = jnp.zeros((bp, CONV_HIST_PAD, CONV_WIDTH), F32)

    xp = x_prompt.reshape(bp * sp, d)
    xs = x_sample.reshape(bs * ss, d)
    small = {name: [] for name in ("pp", "cp", "ps", "cs")}
    stacks = {"p": (None, None), "s": (None, None)}

    def layer_step(x, b, seq, l, group, hist_pool, hist_conv, pos0, mk, mv, mem_tm):
        lam_init = 0.8 - 0.6 * math.exp(-0.3 * l)
        prompt = group == "p"
        x = _ffn(x, *ffn1, l, tm=512, tf=512)
        proj = _in_proj(x, mix_pre_g, w_in, l, *stacks[group], n_layers=n_layers, seq=seq, tm=512, head_major=prompt)
        q, k_stack, v_stack = proj[:3]
        stacks[group] = (k_stack, v_stack)
        zp, cb, cc, cx = (a.reshape(b, seq, -1) for a in proj[-4:])
        y_pool, y_conv, new_conv = _pool_conv(zp, cb, cc, cx, hist_pool, hist_conv, pool_w, pool_scale, conv_w, l,
                                              pos0=pos0)
        q3 = q.reshape(b, seq, DIFF_WIDTH)
        if prompt:
            y_diff = _diff_attn_prompt(q3, proj[3], proj[4], slopes, *lam_vecs, l, tq=512, tk=256, lam_init=lam_init)
        else:
            rows5 = (n_layers, b, seq, N_DIFF_HEADS, DIFF_V_DIM)
            y_diff = _diff_attn_sample(q3, k_stack.reshape(rows5), v_stack.reshape(rows5), cache_diff_k, cache_diff_v,
                                       slopes, *lam_vecs, l, tk=min(1024, past), lam_init=lam_init)
        x = _out_proj(y_pool.reshape(b * seq, -1), y_diff.reshape(b * seq, -1), y_conv.reshape(b * seq, -1),
                      w_out, x, mix_post_g, l, tm=512)
        x = _mem_attn(x.reshape(b, seq, d), mem_pre_g, w_mq, mk, mv, w_mo, mem_post_g, l, l, tm=mem_tm)
        x = _ffn(x.reshape(b * seq, d), *ffn2, l, tm=512, tf=512)
        return x, zp[:, seq - POOL_HIST:, :], new_conv

    for l in range(n_layers):
        xp, pp, cp = layer_step(xp, bp, sp, l, "p", zero_pool, zero_conv, 0, mk_p, mv_p, 512)
        xs, ps, cs = layer_step(xs, bs, ss, l, "s", _pad_front(state_pool[l], POOL_HIST_PAD),
                                _pad_front(state_conv[l], CONV_HIST_PAD), past, mk_s, mv_s, ss)
        for name, val in zip(("pp", "cp", "ps", "cs"), (pp, cp, ps, cs)):
            small[name].append(val)

    st = lambda name: jnp.stack(small[name])
    rows_p = (n_layers, bp, sp, N_DIFF_HEADS, DIFF_V_DIM)
    rows_s = (n_layers, bs, ss, N_DIFF_HEADS, DIFF_V_DIM)
    head_shape = (n_layers, bp, mem_len, N_MEM_HEADS, MEM_HEAD_DIM)
    return (xp.reshape(bp, sp, d), xs.reshape(bs, ss, d),
            stacks["p"][0].reshape(rows_p), stacks["p"][1].reshape(rows_p), st("pp"), st("cp"),
            mk_p.reshape(head_shape), mv_p.reshape(head_shape),
            stacks["s"][0].reshape(rows_s), stacks["s"][1].reshape(rows_s), st("ps"), st("cs"))
```
